```python
import jax, jax.numpy as jnp
from jax import lax
import numpy as np

D_MODEL = 1024
BATCH = 8
SEQ = 2048
DEPTH = 4

EPS = 1e-6
F_FLOOR = 1e-30
GDN_HEADS = 4
GDN_HEAD_DIM = 128
GDN_WIDTH = GDN_HEADS * GDN_HEAD_DIM
GDN_CONV = 4
GDN_CHUNK = 64
HGRN_HEADS = 4
HGRN_STATE = 128
HGRN_HEAD_DIM = 128
HGRN_KEY_WIDTH = HGRN_HEADS * HGRN_STATE
HGRN_VAL_WIDTH = HGRN_HEADS * HGRN_HEAD_DIM
HGRN_CHUNK = 16
MIX_WIDTH = GDN_WIDTH + HGRN_VAL_WIDTH
AB_SIZES = (3 * GDN_WIDTH, GDN_WIDTH, GDN_HEADS, GDN_HEADS,
            HGRN_KEY_WIDTH, HGRN_KEY_WIDTH, HGRN_VAL_WIDTH, HGRN_VAL_WIDTH)
AB_COLS = sum(AB_SIZES)
LRU_WIDTH = D_MODEL
LRU_HEADS = 4
LRU_BLOCK = LRU_WIDTH // LRU_HEADS
LRU_CONV = 4
RG_C = 8.0
D_FF = 2816
FFN_CONV = 3
N_EVEN = (DEPTH + 1) // 2
N_ODD = DEPTH // 2

kernel_name = 'hybrid_gdn_hgrn2_rglru_convffn'


def _rmsnorm(x, gain):
    x32 = x.astype(jnp.float32)
    y = x32 * lax.rsqrt(jnp.mean(x32 * x32, axis=-1, keepdims=True) + EPS)
    return (y * gain.astype(jnp.float32)).astype(x.dtype)


def _l2norm(x):
    return x * lax.rsqrt(jnp.sum(x * x, axis=-1, keepdims=True) + EPS)


def _causal_dwconv(x, w):
    width, ch = w.shape
    return lax.conv_general_dilated(
        x, w[:, None, :].astype(x.dtype), window_strides=(1,), padding=[(width - 1, 0)],
        dimension_numbers=('NWC', 'WIO', 'NWC'), feature_group_count=ch)


def _heads(t, nh):
    b, s, _ = t.shape
    return t.reshape(b, s, nh, -1).transpose(0, 2, 1, 3)


def _masked_exp(logits, mask):
    return jnp.where(mask, jnp.exp(jnp.where(mask, logits, 0.0)), 0.0)


def _gated_delta_rule(q, k, v, g, beta):
    bsz, nh, seq, dk = q.shape
    dv = v.shape[-1]
    c = GDN_CHUNK
    n = seq // c
    q, k = (t.reshape(bsz, nh, n, c, dk) for t in (q, k))
    v = v.reshape(bsz, nh, n, c, dv)
    g, beta = (t.reshape(bsz, nh, n, c) for t in (g, beta))
    gc = jnp.cumsum(g, axis=-1)
    causal = jnp.tril(jnp.ones((c, c), dtype=bool))
    decay = _masked_exp(gc[..., :, None] - gc[..., None, :], causal)
    k_beta = k * beta[..., None]
    lower = jnp.tril(jnp.einsum('bhnik,bhnjk->bhnij', k_beta, k) * decay, -1)
    rhs = jnp.concatenate([v * beta[..., None], k_beta * jnp.exp(gc)[..., None]], axis=-1)
    sol = lax.linalg.triangular_solve(lower + jnp.eye(c, dtype=q.dtype), rhs, left_side=True,
                                      lower=True, unit_diagonal=True)
    u, w = sol[..., :dv], sol[..., dv:]
    attn = jnp.einsum('bhnik,bhnjk->bhnij', q, k) * decay
    q_dec = q * jnp.exp(gc)[..., None]
    k_dec = k * jnp.exp(gc[..., -1:] - gc)[..., None]
    chunk_decay = jnp.exp(gc[..., -1])

    def step(state, xs):
        u_c, w_c, attn_c, q_c, k_c, d_c = xs
        v_new = u_c - jnp.einsum('bhck,bhkv->bhcv', w_c, state)
        o_c = (jnp.einsum('bhck,bhkv->bhcv', q_c, state)
               + jnp.einsum('bhij,bhjv->bhiv', attn_c, v_new))
        state = state * d_c[..., None, None] + jnp.einsum('bhck,bhcv->bhkv', k_c, v_new)
        return state, o_c

    xs = tuple(jnp.moveaxis(t, 2, 0) for t in (u, w, attn, q_dec, k_dec, chunk_decay))
    _, o = lax.scan(step, jnp.zeros((bsz, nh, dk, dv), q.dtype), xs)
    return jnp.moveaxis(o, 0, 2).reshape(bsz, nh, seq, dv)


def _chunk_gla(q, k, v, log_f):
    bsz, nh, seq, dk = q.shape
    dv = v.shape[-1]
    c = HGRN_CHUNK
    n = seq // c
    q, k, log_f = (t.reshape(bsz, nh, n, c, dk) for t in (q, k, log_f))
    v = v.reshape(bsz, nh, n, c, dv)
    b = jnp.cumsum(log_f, axis=3)
    causal = jnp.tril(jnp.ones((c, c), dtype=bool))[:, :, None]
    rel = _masked_exp(b[..., :, None, :] - b[..., None, :, :], causal)
    scores = jnp.sum(q[..., :, None, :] * k[..., None, :, :] * rel, axis=-1)
    o_intra = jnp.einsum('bhnij,bhnjv->bhniv', scores, v)
    b_last = b[..., -1:, :]
    q_dec = q * jnp.exp(b)
    k_dec = k * jnp.exp(b_last - b)
    chunk_decay = jnp.exp(b_last[..., 0, :])

    def step(state, xs):
        q_c, k_c, v_c, d_c = xs
        o_c = jnp.einsum('bhck,bhkv->bhcv', q_c, state)
        state = state * d_c[..., None] + jnp.einsum('bhck,bhcv->bhkv', k_c, v_c)
        return state, o_c

    xs = tuple(jnp.moveaxis(t, 2, 0) for t in (q_dec, k_dec, v, chunk_decay))
    _, o_inter = lax.scan(step, jnp.zeros((bsz, nh, dk, dv), q.dtype), xs)
    return (o_intra + jnp.moveaxis(o_inter, 0, 2)).reshape(bsz, nh, seq, dv)


def _even_mixer(h, w_in, conv_w, a_log, dt_bias, gdn_gain, lower_bound, hgrn_gain, w_out):
    f32 = jnp.float32
    bsz, seq, _ = h.shape
    p = h @ w_in
    offsets = [int(o) for o in np.cumsum(AB_SIZES)[:-1]]
    qkv_a, z_a, beta_a, alpha_a, q_b, f_b, i_b, g_b = jnp.split(p, offsets, axis=-1)
    qkv = jax.nn.silu(_causal_dwconv(qkv_a, conv_w))
    q, k, v = (_heads(t, GDN_HEADS).astype(f32) for t in jnp.split(qkv, 3, axis=-1))
    q = _l2norm(q) * (GDN_HEAD_DIM ** -0.5)
    k = _l2norm(k)
    beta = jax.nn.sigmoid(beta_a.astype(f32)).transpose(0, 2, 1)
    g = (-jnp.exp(a_log.astype(f32))
         * jax.nn.softplus(alpha_a.astype(f32) + dt_bias.astype(f32))).transpose(0, 2, 1)
    o_a = _gated_delta_rule(q, k, v, g, beta).transpose(0, 2, 1, 3)
    z = jax.nn.silu(z_a.astype(f32)).reshape(bsz, seq, GDN_HEADS, GDN_HEAD_DIM)
    o_a = _rmsnorm(o_a, gdn_gain) * z
    f = lower_bound + (1.0 - lower_bound) * jax.nn.sigmoid(f_b.astype(f32))
    log_f = jnp.log(jnp.maximum(f, F_FLOOR))
    k_b = 1.0 - f
    q_b = jax.nn.silu(q_b.astype(f32))
    o_b = _chunk_gla(_heads(q_b, HGRN_HEADS), _heads(k_b, HGRN_HEADS),
                     _heads(i_b.astype(f32), HGRN_HEADS), _heads(log_f, HGRN_HEADS))
    o_b = o_b.transpose(0, 2, 1, 3)
    gate_b = jax.nn.silu(g_b.astype(f32)).reshape(bsz, seq, HGRN_HEADS, HGRN_HEAD_DIM)
    o_b = _rmsnorm(o_b, hgrn_gain) * gate_b
    o = jnp.concatenate([o_a.reshape(bsz, seq, GDN_WIDTH),
                         o_b.reshape(bsz, seq, HGRN_VAL_WIDTH)], axis=-1).astype(h.dtype)
    return o @ w_out


def _rglru_block(h, w_in, conv_w, conv_b, gate_a_w, gate_a_b, gate_x_w, gate_x_b, lam, w_out):
    f32 = jnp.float32
    bsz, seq, _ = h.shape
    y_branch, x_branch = jnp.split(h @ w_in, 2, axis=-1)
    gate = jax.nn.gelu(y_branch.astype(f32), approximate=True)
    xc = (_causal_dwconv(x_branch, conv_w) + conv_b).astype(f32)
    xb = xc.reshape(bsz, seq, LRU_HEADS, LRU_BLOCK)
    r = jax.nn.sigmoid(jnp.einsum('bthi,hij->bthj', xb, gate_a_w.astype(f32)).reshape(bsz, seq, LRU_WIDTH)
                       + gate_a_b.astype(f32))
    i = jax.nn.sigmoid(jnp.einsum('bthi,hij->bthj', xb, gate_x_w.astype(f32)).reshape(bsz, seq, LRU_WIDTH)
                       + gate_x_b.astype(f32))
    log_a = -RG_C * r * jax.nn.softplus(-lam.astype(f32))
    a = jnp.exp(log_a)
    u = jnp.sqrt(jnp.maximum(-jnp.expm1(2.0 * log_a), 0.0)) * (i * xc)

    def combine(left, right):
        a_l, b_l = left
        a_r, b_r = right
        return a_l * a_r, a_r * b_l + b_r

    _, hs = lax.associative_scan(combine, (a, u), axis=1)
    return (hs * gate).astype(h.dtype) @ w_out


def _conv_ffn(h, w_up, conv_w, conv_b, w_down):
    gate, val = jnp.split(h @ w_up, 2, axis=-1)
    gate = _causal_dwconv(gate, conv_w) + conv_b
    return (jax.nn.silu(gate) * val) @ w_down


def setup_inputs(seed: int = 0) -> dict:
    key = jax.random.key(seed)
    ks = iter(jax.random.split(key, 40))
    f32 = jnp.float32

    def nrm(shape, scale):
        return jax.random.normal(next(ks), shape, f32) * scale

    def uni(shape, lo, hi):
        return jax.random.uniform(next(ks), shape, f32, lo, hi)

    x = nrm((BATCH, SEQ, D_MODEL), 1.0)
    norm_mix = 1.0 + nrm((DEPTH, D_MODEL), 0.02)
    norm_ffn = 1.0 + nrm((DEPTH, D_MODEL), 0.02)
    norm_final = 1.0 + nrm((D_MODEL,), 0.02)
    ab_w_in = nrm((N_EVEN, D_MODEL, AB_COLS), D_MODEL ** -0.5)
    gdn_conv_w = nrm((N_EVEN, GDN_CONV, 3 * GDN_WIDTH), GDN_CONV ** -0.5)
    gdn_a_log = jnp.log(uni((N_EVEN, GDN_HEADS), 1.0, 16.0))
    dt = jnp.exp(uni((N_EVEN, GDN_HEADS), float(np.log(1e-3)), float(np.log(1e-1))))
    gdn_dt_bias = dt + jnp.log(-jnp.expm1(-dt))
    gdn_norm = 1.0 + nrm((N_EVEN, GDN_HEAD_DIM), 0.02)
    hgrn_lower_bounds = 1.0 + nrm((N_EVEN, HGRN_KEY_WIDTH), 0.1)
    hgrn_norm = 1.0 + nrm((N_EVEN, HGRN_HEAD_DIM), 0.02)
    ab_w_out = nrm((N_EVEN, MIX_WIDTH, D_MODEL), MIX_WIDTH ** -0.5)
    c_w_in = nrm((N_ODD, D_MODEL, 2 * LRU_WIDTH), D_MODEL ** -0.5)
    c_conv_w = nrm((N_ODD, LRU_CONV, LRU_WIDTH), LRU_CONV ** -0.5)
    c_conv_b = nrm((N_ODD, LRU_WIDTH), 0.01)
    c_gate_a_w = nrm((N_ODD, LRU_HEADS, LRU_BLOCK, LRU_BLOCK), LRU_BLOCK ** -0.5)
    c_gate_a_b = nrm((N_ODD, LRU_WIDTH), 0.01)
    c_gate_x_w = nrm((N_ODD, LRU_HEADS, LRU_BLOCK, LRU_BLOCK), LRU_BLOCK ** -0.5)
    c_gate_x_b = nrm((N_ODD, LRU_WIDTH), 0.01)
    a_c = uni((N_ODD, LRU_WIDTH), 0.9, 0.999) ** (1.0 / RG_C)
    c_lambda = jnp.log(a_c) - jnp.log1p(-a_c)
    c_w_out = nrm((N_ODD, LRU_WIDTH, D_MODEL), LRU_WIDTH ** -0.5)
    ffn_w_up = nrm((DEPTH, D_MODEL, 2 * D_FF), D_MODEL ** -0.5)
    ffn_conv_w = nrm((DEPTH, FFN_CONV, D_FF), FFN_CONV ** -0.5)
    ffn_conv_b = nrm((DEPTH, D_FF), 0.01)
    ffn_w_down = nrm((DEPTH, D_FF, D_MODEL), D_FF ** -0.5)
    return {'x': x, 'norm_mix': norm_mix, 'norm_ffn': norm_ffn, 'norm_final': norm_final,
            'ab_w_in': ab_w_in, 'gdn_conv_w': gdn_conv_w, 'gdn_a_log': gdn_a_log,
            'gdn_dt_bias': gdn_dt_bias, 'gdn_norm': gdn_norm, 'hgrn_lower_bounds': hgrn_lower_bounds,
            'hgrn_norm': hgrn_norm, 'ab_w_out': ab_w_out, 'c_w_in': c_w_in, 'c_conv_w': c_conv_w,
            'c_conv_b': c_conv_b, 'c_gate_a_w': c_gate_a_w, 'c_gate_a_b': c_gate_a_b,
            'c_gate_x_w': c_gate_x_w, 'c_gate_x_b': c_gate_x_b, 'c_lambda': c_lambda,
            'c_w_out': c_w_out, 'ffn_w_up': ffn_w_up, 'ffn_conv_w': ffn_conv_w,
            'ffn_conv_b': ffn_conv_b, 'ffn_w_down': ffn_w_down}


def reference(x, norm_mix, norm_ffn, norm_final, ab_w_in, gdn_conv_w, gdn_a_log, gdn_dt_bias,
              gdn_norm, hgrn_lower_bounds, hgrn_norm, ab_w_out, c_w_in, c_conv_w, c_conv_b,
              c_gate_a_w, c_gate_a_b, c_gate_x_w, c_gate_x_b, c_lambda, c_w_out,
              ffn_w_up, ffn_conv_w, ffn_conv_b, ffn_w_down):
    lb_p = jax.nn.softmax(hgrn_lower_bounds.astype(jnp.float32), axis=0)
    lower_bounds = jnp.cumsum(lb_p, axis=0) - lb_p[0]
    for layer in range(DEPTH):
        j = layer // 2
        h = _rmsnorm(x, norm_mix[layer])
        if layer % 2 == 0:
            mix = _even_mixer(h, ab_w_in[j], gdn_conv_w[j], gdn_a_log[j], gdn_dt_bias[j],
                              gdn_norm[j], lower_bounds[j], hgrn_norm[j], ab_w_out[j])
        else:
            mix = _rglru_block(h, c_w_in[j], c_conv_w[j], c_conv_b[j], c_gate_a_w[j],
                               c_gate_a_b[j], c_gate_x_w[j], c_gate_x_b[j], c_lambda[j], c_w_out[j])
        x = x + mix
        h = _rmsnorm(x, norm_ffn[layer])
        x = x + _conv_ffn(h, ffn_w_up[layer], ffn_conv_w[layer], ffn_conv_b[layer], ffn_w_down[layer])
    return _rmsnorm(x, norm_final)
```

```python
import functools

import jax
import jax.numpy as jnp
from jax import lax
from jax.experimental import pallas as pl
from jax.experimental.pallas import tpu as pltpu

F32 = jnp.float32
BF16 = jnp.bfloat16
HIGHEST = lax.Precision.HIGHEST

EPS = 1e-6
F_FLOOR = 1e-30
RG_C = 8.0

HEAD_DIM = 128
N_HEADS = 4
GDN_CHUNK = 64
GDN_CONV = 4
HGRN_CHUNK = 16
LRU_BLOCK = 256
LRU_CONV = 4
FFN_CONV = 3
HALO = 8

VMEM_LIMIT_BYTES = 56 * 1024 * 1024


def _params(*semantics):
    return pltpu.CompilerParams(dimension_semantics=semantics,
                                vmem_limit_bytes=VMEM_LIMIT_BYTES)


def _sigmoid(x):
    return 1.0 / (1.0 + jnp.exp(-x))


def _silu(x):
    return x * _sigmoid(x)


def _softplus(x):
    return jnp.maximum(x, 0.0) + jnp.log(1.0 + jnp.exp(-jnp.abs(x)))


def _mm(a, b):
    return jnp.dot(a.astype(BF16), b.astype(BF16), preferred_element_type=F32)


def _mm_nt(a, b):
    return lax.dot_general(a.astype(BF16), b.astype(BF16), (((1,), (1,)), ((), ())),
                           preferred_element_type=F32)


def _mm_tn(a, b):
    return lax.dot_general(a.astype(BF16), b.astype(BF16), (((0,), (0,)), ((), ())),
                           preferred_element_type=F32)


def _mm_hi(a, b):
    return jnp.dot(a, b, preferred_element_type=F32, precision=HIGHEST)


def _mm_nt_hi(a, b):
    return lax.dot_general(a, b, (((1,), (1,)), ((), ())),
                           preferred_element_type=F32, precision=HIGHEST)


def _rms_rows(x, gain):
    ms = jnp.mean(x * x, axis=-1, keepdims=True)
    return x * lax.rsqrt(ms + EPS) * gain


def _causal_conv(ext, w, width):
    out = ext[HALO:] * w[width - 1:width]
    for k in range(width - 1):
        shift = width - 1 - k
        out = out + pltpu.roll(ext, shift, 0)[HALO:] * w[k:k + 1]
    return out


def _norm_mm_kernel(x_ref, g_ref, w_ref, *rest, has_small):
    if has_small:
        ws_ref, o_ref, os_ref, h_ref = rest
    else:
        o_ref, h_ref = rest

    @pl.when(pl.program_id(1) == 0)
    def _():
        h = _rms_rows(x_ref[...], g_ref[...]).astype(BF16)
        h_ref[...] = h
        if has_small:
            os_ref[...] = jnp.dot(h, ws_ref[...], preferred_element_type=F32)

    o_ref[...] = jnp.dot(h_ref[...], w_ref[...], preferred_element_type=F32)


def _norm_matmul(x, gain, w, w_small=None, *, tm=1024, tn=512):
    m, d = x.shape
    n = w.shape[1]
    has_small = w_small is not None
    in_specs = [pl.BlockSpec((tm, d), lambda i, j: (i, 0)),
                pl.BlockSpec((1, d), lambda i, j: (0, 0)),
                pl.BlockSpec((d, tn), lambda i, j: (0, j))]
    out_specs = [pl.BlockSpec((tm, tn), lambda i, j: (i, j))]
    out_shape = [jax.ShapeDtypeStruct((m, n), F32)]
    args = [x, gain.reshape(1, d), w]
    if has_small:
        ns = w_small.shape[1]
        in_specs.append(pl.BlockSpec((d, ns), lambda i, j: (0, 0)))
        out_specs.append(pl.BlockSpec((tm, ns), lambda i, j: (i, 0)))
        out_shape.append(jax.ShapeDtypeStruct((m, ns), F32))
        args.append(w_small)
    outs = pl.pallas_call(
        functools.partial(_norm_mm_kernel, has_small=has_small),
        grid=(m // tm, n // tn),
        in_specs=in_specs, out_specs=out_specs, out_shape=out_shape,
        scratch_shapes=[pltpu.VMEM((tm, d), BF16)],
        compiler_params=_params("parallel", "arbitrary"),
        name="norm_matmul",
    )(*args)
    return outs if has_small else outs[0]


def _mm_res_kernel(*refs, n_lhs):
    res_ref = refs[0]
    a_refs = refs[1:1 + n_lhs]
    w_refs = refs[1 + n_lhs:1 + 2 * n_lhs]
    o_ref = refs[1 + 2 * n_lhs]
    acc = res_ref[...]
    for a_ref, w_ref in zip(a_refs, w_refs):
        acc = acc + jnp.dot(a_ref[...], w_ref[...], preferred_element_type=F32)
    o_ref[...] = acc


def _matmul_residual(res, lhs_list, w_list, *, tm=512):
    m, n = res.shape
    n_lhs = len(lhs_list)
    in_specs = [pl.BlockSpec((tm, n), lambda i: (i, 0))]
    in_specs += [pl.BlockSpec((tm, a.shape[1]), lambda i: (i, 0)) for a in lhs_list]
    in_specs += [pl.BlockSpec(w.shape, lambda i: (0, 0)) for w in w_list]
    return pl.pallas_call(
        functools.partial(_mm_res_kernel, n_lhs=n_lhs),
        grid=(m // tm,),
        in_specs=in_specs,
        out_specs=pl.BlockSpec((tm, n), lambda i: (i, 0)),
        out_shape=jax.ShapeDtypeStruct((m, n), F32),
        compiler_params=_params("parallel"),
        name="matmul_residual",
    )(res, *lhs_list, *w_list)


def _inv_unit_lower(low, ii, jj):
    eye = (ii == jj).astype(F32)
    same_block = (ii // 16) == (jj // 16)
    d1 = jnp.where(same_block, low, 0.0)
    e = jnp.where(same_block, 0.0, low)
    d2 = _mm_hi(d1, d1)
    d4 = _mm_hi(d2, d2)
    d8 = _mm_hi(d4, d4)
    dinv = _mm_hi(eye - d1, eye + d2)
    dinv = _mm_hi(dinv, eye + d4)
    dinv = _mm_hi(dinv, eye + d8)
    n1 = _mm_hi(dinv, e)
    n2 = _mm_hi(n1, n1)
    return _mm_hi(_mm_hi(eye - n1, eye + n2), dinv)


def _gdn_kernel(qkv_ref, z_ref, s_ref, cw_ref, alog_ref, dt_ref, gain_ref, o_ref,
                state_ref, halo_ref, *, tc):
    c = GDN_CHUNK
    hd = HEAD_DIM
    width = N_HEADS * hd

    @pl.when(pl.program_id(1) == 0)
    def _():
        state_ref[...] = jnp.zeros_like(state_ref)
        halo_ref[...] = jnp.zeros_like(halo_ref)

    ii = lax.broadcasted_iota(jnp.int32, (c, c), 0)
    jj = lax.broadcasted_iota(jnp.int32, (c, c), 1)
    causal = ii >= jj
    strict = ii > jj
    tri_incl = causal.astype(F32)
    lane = lax.broadcasted_iota(jnp.int32, (c, hd), 1)
    neg_a = -jnp.exp(alog_ref[...])
    dt = dt_ref[...]
    gain = gain_ref[...]

    def chunk_body(ci, carry):
        r0 = pl.multiple_of(ci * c, c)
        rp = pl.multiple_of(jnp.maximum(r0 - HALO, 0), HALO)
        first = ci == 0
        small = s_ref[pl.ds(r0, c), :]
        beta_all = _sigmoid(small)
        g_all = neg_a * _softplus(small + dt)
        gc_all = _mm_hi(tri_incl, g_all)
        for h in range(N_HEADS):
            def conv_in(col):
                cur = qkv_ref[pl.ds(r0, c), col:col + hd]
                prev = jnp.where(first, halo_ref[:, col:col + hd],
                                 qkv_ref[pl.ds(rp, HALO), col:col + hd])
                ext = jnp.concatenate([prev, cur], axis=0)
                return _silu(_causal_conv(ext, cw_ref[:, col:col + hd], GDN_CONV))

            q = conv_in(h * hd)
            k = conv_in(width + h * hd)
            v = conv_in(2 * width + h * hd)
            q = q * lax.rsqrt(jnp.sum(q * q, axis=-1, keepdims=True) + EPS) * (hd ** -0.5)
            k = k * lax.rsqrt(jnp.sum(k * k, axis=-1, keepdims=True) + EPS)

            beta = jnp.sum(jnp.where(lane == h, beta_all, 0.0), axis=-1, keepdims=True)
            sel = (lane == N_HEADS + h).astype(F32)
            gc_col = jnp.sum(gc_all * sel, axis=-1, keepdims=True)
            gc_row = _mm_nt_hi(sel, gc_all)
            decay = jnp.where(causal, jnp.exp(jnp.where(causal, gc_col - gc_row, 0.0)), 0.0)
            exp_gc = jnp.exp(gc_col)
            gc_last = gc_col[c - 1:c, :]
            k_beta = k * beta
            low = jnp.where(strict, _mm_nt(k_beta, k) * decay, 0.0)
            t_inv = _inv_unit_lower(low, ii, jj)
            u = _mm_hi(t_inv, v * beta)
            w = _mm_hi(t_inv, k_beta * exp_gc)
            attn = _mm_nt(q, k) * decay
            q_dec = q * exp_gc
            k_dec = k * jnp.exp(gc_last - gc_col)

            state = state_ref[h]
            v_new = u - _mm(w, state)
            o = _mm(q_dec, state) + _mm(attn, v_new)
            state_ref[h] = state * jnp.exp(gc_last) + _mm_tn(k_dec, v_new)

            zg = _silu(z_ref[pl.ds(r0, c), h * hd:(h + 1) * hd])
            o_ref[pl.ds(r0, c), h * hd:(h + 1) * hd] = (_rms_rows(o, gain) * zg).astype(BF16)
        return carry

    lax.fori_loop(0, tc // c, chunk_body, 0)
    halo_ref[...] = qkv_ref[tc - HALO:tc, :]


def _gdn(p_main, p_small, conv_w, a_vec, dt_vec, gain, *, batch, seq, tc=512):
    m = batch * seq
    nt = seq // tc
    width = N_HEADS * HEAD_DIM
    row = lambda b, t: (b * nt + t, 0)
    return pl.pallas_call(
        functools.partial(_gdn_kernel, tc=tc),
        grid=(batch, nt),
        in_specs=[pl.BlockSpec((tc, 3 * width), row),
                  pl.BlockSpec((tc, width), lambda b, t: (b * nt + t, 3)),
                  pl.BlockSpec((tc, HEAD_DIM), row),
                  pl.BlockSpec((GDN_CONV, 3 * width), lambda b, t: (0, 0)),
                  pl.BlockSpec((1, HEAD_DIM), lambda b, t: (0, 0)),
                  pl.BlockSpec((1, HEAD_DIM), lambda b, t: (0, 0)),
                  pl.BlockSpec((1, HEAD_DIM), lambda b, t: (0, 0))],
        out_specs=pl.BlockSpec((tc, width), row),
        out_shape=jax.ShapeDtypeStruct((m, width), BF16),
        scratch_shapes=[pltpu.VMEM((N_HEADS, HEAD_DIM, HEAD_DIM), F32),
                        pltpu.VMEM((HALO, 3 * width), F32)],
        compiler_params=_params("parallel", "arbitrary"),
        name="gdn",
    )(p_main, p_main, p_small, conv_w, a_vec, dt_vec, gain)


def _hgrn_kernel(q_ref, f_ref, i_ref, g_ref, lbp_ref, gain_ref, o_ref, state_ref,
                 *, tc, layer, sup):
    c = HGRN_CHUNK
    hd = HEAD_DIM
    n_sub = sup // c

    @pl.when(pl.program_id(2) == 0)
    def _():
        state_ref[...] = jnp.zeros_like(state_ref)

    lbp = lbp_ref[...]
    lbe = jnp.exp(lbp - jnp.max(lbp, axis=0, keepdims=True))
    lbs = lbe / jnp.sum(lbe, axis=0, keepdims=True)
    lb = jnp.sum(lbs[0:layer + 1], axis=0, keepdims=True) - lbs[0:1]
    gain = gain_ref[...]

    ri = lax.broadcasted_iota(jnp.int32, (sup, sup), 0)
    rj = lax.broadcasted_iota(jnp.int32, (sup, sup), 1)
    same = (ri // c) == (rj // c)
    cum_mat = (same & (ri >= rj)).astype(F32)
    tot_mat = same.astype(F32)
    sub_row = lax.broadcasted_iota(jnp.int32, (c, hd), 0)
    gi = lax.broadcasted_iota(jnp.int32, (c, c * c), 0)
    gj = lax.broadcasted_iota(jnp.int32, (c, c * c), 1)
    group = ((gj // c) == gi).astype(BF16)
    ones = jnp.ones((hd, hd), BF16)

    def sup_body(si, carry):
        r0 = pl.multiple_of(si * sup, sup)
        f = lb + (1.0 - lb) * _sigmoid(f_ref[pl.ds(r0, sup), :])
        log_f = jnp.log(jnp.maximum(f, F_FLOOR))
        kk = 1.0 - f
        q = _silu(q_ref[pl.ds(r0, sup), :])
        v = i_ref[pl.ds(r0, sup), :]
        b = _mm_hi(cum_mat, log_f)
        b_tot = _mm_hi(tot_mat, log_f)
        e_b = jnp.exp(b)
        q_dec = q * e_b
        k_dec = kk * jnp.exp(b_tot - b)
        outs = []
        for s in range(n_sub):
            lo = s * c
            b_c = b[lo:lo + c]
            kk_c = kk[lo:lo + c]
            v_c = v[lo:lo + c]
            pairs = []
            for i in range(c):
                diff = jnp.where(sub_row <= i, b_c[i:i + 1] - b_c, -1e30)
                pairs.append(jnp.exp(diff) * kk_c * q[lo + i:lo + i + 1])
            pair = jnp.concatenate(pairs, axis=0)
            scores = _mm(pair, ones)
            v_rep = jnp.concatenate([v_c] * c, axis=0)
            o_intra = _mm(group, scores * v_rep)
            st = state_ref[...]
            o_inter = _mm_nt(q_dec[lo:lo + c], st)
            state_ref[...] = st * e_b[lo + c - 1:lo + c] + _mm_tn(v_c, k_dec[lo:lo + c])
            outs.append(o_intra + o_inter)
        o = jnp.concatenate(outs, axis=0)
        gate = _silu(g_ref[pl.ds(r0, sup), :])
        o_ref[pl.ds(r0, sup), :] = (_rms_rows(o, gain) * gate).astype(BF16)
        return carry

    lax.fori_loop(0, tc // sup, sup_body, 0)


def _hgrn(p_main, lower_bounds_raw, gain, *, layer, batch, seq, tc=512, sup=128):
    m = batch * seq
    nt = seq // tc
    n_layers = lower_bounds_raw.shape[0]
    col = lambda base: (lambda b, h, t: (b * nt + t, base + h))
    return pl.pallas_call(
        functools.partial(_hgrn_kernel, tc=tc, layer=layer, sup=sup),
        grid=(batch, N_HEADS, nt),
        in_specs=[pl.BlockSpec((tc, HEAD_DIM), col(16)),
                  pl.BlockSpec((tc, HEAD_DIM), col(20)),
                  pl.BlockSpec((tc, HEAD_DIM), col(24)),
                  pl.BlockSpec((tc, HEAD_DIM), col(28)),
                  pl.BlockSpec((n_layers, HEAD_DIM), lambda b, h, t: (0, h)),
                  pl.BlockSpec((1, HEAD_DIM), lambda b, h, t: (0, 0))],
        out_specs=pl.BlockSpec((tc, HEAD_DIM), col(0)),
        out_shape=jax.ShapeDtypeStruct((m, N_HEADS * HEAD_DIM), BF16),
        scratch_shapes=[pltpu.VMEM((HEAD_DIM, HEAD_DIM), F32)],
        compiler_params=_params("parallel", "parallel", "arbitrary"),
        name="hgrn",
    )(p_main, p_main, p_main, p_main, lower_bounds_raw, gain)


def _rglru_kernel(y_ref, x_ref, cw_ref, cb_ref, wa_ref, ba_ref, wx_ref, bx_ref, lam_ref,
                  o_ref, h_ref, halo_ref, *, tc):
    n_heads = wa_ref.shape[0]
    blk = LRU_BLOCK

    @pl.when(pl.program_id(1) == 0)
    def _():
        h_ref[...] = jnp.zeros_like(h_ref)
        halo_ref[...] = jnp.zeros_like(halo_ref)

    x = x_ref[...]
    ext = jnp.concatenate([halo_ref[...], x], axis=0)
    halo_ref[...] = x[tc - HALO:tc]
    xc = _causal_conv(ext, cw_ref[...], LRU_CONV) + cb_ref[...]
    ra = jnp.concatenate([_mm(xc[:, h * blk:(h + 1) * blk], wa_ref[h]) for h in range(n_heads)],
                         axis=1)
    rx = jnp.concatenate([_mm(xc[:, h * blk:(h + 1) * blk], wx_ref[h]) for h in range(n_heads)],
                         axis=1)
    r = _sigmoid(ra + ba_ref[...])
    gi = _sigmoid(rx + bx_ref[...])
    log_a = -RG_C * r * _softplus(-lam_ref[...])
    a = jnp.exp(log_a)
    th = jnp.tanh(log_a)
    one_minus_a2 = -2.0 * th / (1.0 - th)
    u = jnp.sqrt(jnp.maximum(one_minus_a2, 0.0)) * (gi * xc)

    sub = lax.broadcasted_iota(jnp.int32, a.shape, 0) % HALO
    for s in (1, 2, 4):
        a_prev = jnp.where(sub >= s, pltpu.roll(a, s, 0), 1.0)
        u_prev = jnp.where(sub >= s, pltpu.roll(u, s, 0), 0.0)
        u = a * u_prev + u
        a = a * a_prev
    carry = h_ref[0:1, :]
    tiles = []
    for t in range(tc // HALO):
        ht = a[t * HALO:(t + 1) * HALO] * carry + u[t * HALO:(t + 1) * HALO]
        tiles.append(ht)
        carry = ht[HALO - 1:HALO]
    h_ref[...] = jnp.broadcast_to(carry, h_ref.shape)
    hs = jnp.concatenate(tiles, axis=0)
    y = y_ref[...]
    gate = 0.5 * y * (1.0 + jnp.tanh(0.7978845608028654 * (y + 0.044715 * (y * y * y))))
    o_ref[...] = (hs * gate).astype(BF16)


def _rglru(pc, conv_w, conv_b, wa, ba, wx, bx, lam, *, batch, seq, tc=256):
    m = batch * seq
    nt = seq // tc
    width = conv_w.shape[1]
    vec = pl.BlockSpec((1, width), lambda b, t: (0, 0))
    mat = pl.BlockSpec(wa.shape, lambda b, t: (0, 0, 0))
    return pl.pallas_call(
        functools.partial(_rglru_kernel, tc=tc),
        grid=(batch, nt),
        in_specs=[pl.BlockSpec((tc, width), lambda b, t: (b * nt + t, 0)),
                  pl.BlockSpec((tc, width), lambda b, t: (b * nt + t, 1)),
                  pl.BlockSpec((LRU_CONV, width), lambda b, t: (0, 0)),
                  vec, mat, vec, mat, vec, vec],
        out_specs=pl.BlockSpec((tc, width), lambda b, t: (b * nt + t, 0)),
        out_shape=jax.ShapeDtypeStruct((m, width), BF16),
        scratch_shapes=[pltpu.VMEM((HALO, width), F32), pltpu.VMEM((HALO, width), F32)],
        compiler_params=_params("parallel", "arbitrary"),
        name="rglru",
    )(pc, pc, conv_w, conv_b.reshape(1, width), wa, ba.reshape(1, width), wx,
      bx.reshape(1, width), lam.reshape(1, width))


def _ffn_kernel(x_ref, g_ref, wg_ref, wv_ref, cw_ref, cb_ref, wd_ref, gf_ref, o_ref,
                h_ref, act_ref, halo_ref, *, tiles_per_seq, n_chunks, tf, final_norm):
    i = pl.program_id(0)
    j = pl.program_id(1)

    @pl.when(j == 0)
    def _():
        h_ref[...] = _rms_rows(x_ref[...], g_ref[...]).astype(BF16)

    @pl.when((j == 0) & (i % tiles_per_seq == 0))
    def _():
        halo_ref[...] = jnp.zeros_like(halo_ref)

    h = h_ref[...]
    gate = jnp.dot(h, wg_ref[...], preferred_element_type=F32)
    val = jnp.dot(h, wv_ref[...], preferred_element_type=F32)
    ext = jnp.concatenate([halo_ref[j], gate], axis=0)
    halo_ref[j] = gate[gate.shape[0] - HALO:]
    gc = _causal_conv(ext, cw_ref[...], FFN_CONV) + cb_ref[...]
    act_ref[j] = (_silu(gc) * val).astype(BF16)

    @pl.when(j == n_chunks - 1)
    def _():
        acc = x_ref[...]
        for cidx in range(n_chunks):
            acc = acc + jnp.dot(act_ref[cidx], wd_ref[cidx * tf:(cidx + 1) * tf, :],
                                preferred_element_type=F32)
        if final_norm:
            acc = _rms_rows(acc, gf_ref[...])
        o_ref[...] = acc


def _ffn(x, gain, w_up, conv_w, conv_b, w_down, final_gain, *, seq, final_norm, tm=1024, tf=256):
    m, d = x.shape
    d_ff = w_down.shape[0]
    n_chunks = d_ff // tf
    return pl.pallas_call(
        functools.partial(_ffn_kernel, tiles_per_seq=seq // tm, n_chunks=n_chunks, tf=tf,
                          final_norm=final_norm),
        grid=(m // tm, n_chunks),
        in_specs=[pl.BlockSpec((tm, d), lambda i, j: (i, 0)),
                  pl.BlockSpec((1, d), lambda i, j: (0, 0)),
                  pl.BlockSpec((d, tf), lambda i, j: (0, j)),
                  pl.BlockSpec((d, tf), lambda i, j: (0, n_chunks + j)),
                  pl.BlockSpec((FFN_CONV, tf), lambda i, j: (0, j)),
                  pl.BlockSpec((1, tf), lambda i, j: (0, j)),
                  pl.BlockSpec((d_ff, d), lambda i, j: (0, 0)),
                  pl.BlockSpec((1, d), lambda i, j: (0, 0))],
        out_specs=pl.BlockSpec((tm, d), lambda i, j: (i, 0)),
        out_shape=jax.ShapeDtypeStruct((m, d), F32),
        scratch_shapes=[pltpu.VMEM((tm, d), BF16),
                        pltpu.VMEM((n_chunks, tm, tf), BF16),
                        pltpu.VMEM((n_chunks, HALO, tf), F32)],
        compiler_params=_params("arbitrary", "arbitrary"),
        name="conv_ffn",
    )(x, gain.reshape(1, d), w_up, w_up, conv_w, conv_b.reshape(1, d_ff), w_down,
      final_gain.reshape(1, d))


def _even_layer(x, gain, w_in, conv_w, a_log, dt_bias, gdn_gain, lower_bounds_raw, hgrn_gain,
                w_out, *, layer, batch, seq):
    width = N_HEADS * HEAD_DIM
    split = 4 * width
    w_main = jnp.concatenate([w_in[:, :split], w_in[:, split + 2 * N_HEADS:]], axis=1).astype(BF16)
    w_small = jnp.pad(w_in[:, split:split + 2 * N_HEADS],
                      ((0, 0), (0, HEAD_DIM - 2 * N_HEADS))).astype(BF16)
    p_main, p_small = _norm_matmul(x, gain, w_main, w_small)
    pad = lambda vec: jnp.pad(vec.astype(F32), (N_HEADS, HEAD_DIM - 2 * N_HEADS)).reshape(1, HEAD_DIM)
    o_a = _gdn(p_main, p_small, conv_w, pad(a_log), pad(dt_bias), gdn_gain.reshape(1, HEAD_DIM),
               batch=batch, seq=seq)
    o_b = _hgrn(p_main, lower_bounds_raw, hgrn_gain.reshape(1, HEAD_DIM), layer=layer,
                batch=batch, seq=seq)
    w_out = w_out.astype(BF16)
    return _matmul_residual(x, [o_a, o_b], [w_out[:width], w_out[width:]])


def _odd_layer(x, gain, w_in, conv_w, conv_b, wa, ba, wx, bx, lam, w_out, *, batch, seq):
    pc = _norm_matmul(x, gain, w_in.astype(BF16))
    hs = _rglru(pc, conv_w, conv_b, wa.astype(BF16), ba, wx.astype(BF16), bx, lam,
                batch=batch, seq=seq)
    return _matmul_residual(x, [hs], [w_out.astype(BF16)])


def kernel(x, norm_mix, norm_ffn, norm_final, ab_w_in, gdn_conv_w, gdn_a_log, gdn_dt_bias,
           gdn_norm, hgrn_lower_bounds, hgrn_norm, ab_w_out, c_w_in, c_conv_w, c_conv_b,
           c_gate_a_w, c_gate_a_b, c_gate_x_w, c_gate_x_b, c_lambda, c_w_out,
           ffn_w_up, ffn_conv_w, ffn_conv_b, ffn_w_down):
    batch, seq, d = x.shape
    depth = norm_mix.shape[0]
    x = x.reshape(batch * seq, d)
    for layer in range(depth):
        j = layer // 2
        if layer % 2 == 0:
            x = _even_layer(x, norm_mix[layer], ab_w_in[j], gdn_conv_w[j], gdn_a_log[j],
                            gdn_dt_bias[j], gdn_norm[j], hgrn_lower_bounds, hgrn_norm[j],
                            ab_w_out[j], layer=j, batch=batch, seq=seq)
        else:
            x = _odd_layer(x, norm_mix[layer], c_w_in[j], c_conv_w[j], c_conv_b[j],
                           c_gate_a_w[j], c_gate_a_b[j], c_gate_x_w[j], c_gate_x_b[j],
                           c_lambda[j], c_w_out[j], batch=batch, seq=seq)
        x = _ffn(x, norm_ffn[layer], ffn_w_up[layer].astype(BF16), ffn_conv_w[layer],
                 ffn_conv_b[layer], ffn_w_down[layer].astype(BF16), norm_final, seq=seq,
                 final_norm=layer == depth - 1)
    return x.reshape(batch, seq, d)
```

```python
import functools

import jax
import jax.numpy as jnp
from jax import lax
from jax.experimental import pallas as pl
from jax.experimental.pallas import tpu as pltpu

F32 = jnp.float32
BF16 = jnp.bfloat16
HIGHEST = lax.Precision.HIGHEST

EPS = 1e-6
F_FLOOR = 1e-30
RG_C = 8.0

HEAD_DIM = 128
N_HEADS = 4
GDN_CHUNK = 64
GDN_CONV = 4
HGRN_CHUNK = 16
LRU_BLOCK = 256
LRU_CONV = 4
FFN_CONV = 3
HALO = 8

VMEM_LIMIT_BYTES = 56 * 1024 * 1024


def _params(*semantics):
    return pltpu.CompilerParams(dimension_semantics=semantics,
                                vmem_limit_bytes=VMEM_LIMIT_BYTES)


def _sigmoid(x):
    return 1.0 / (1.0 + jnp.exp(-x))


def _silu(x):
    return x * _sigmoid(x)


def _softplus(x):
    return jnp.maximum(x, 0.0) + jnp.log(1.0 + jnp.exp(-jnp.abs(x)))


def _mm(a, b):
    return jnp.dot(a.astype(BF16), b.astype(BF16), preferred_element_type=F32)


def _mm_nt(a, b):
    return lax.dot_general(a.astype(BF16), b.astype(BF16), (((1,), (1,)), ((), ())),
                           preferred_element_type=F32)


def _mm_tn(a, b):
    return lax.dot_general(a.astype(BF16), b.astype(BF16), (((0,), (0,)), ((), ())),
                           preferred_element_type=F32)


def _bmm(a, b):
    return lax.dot_general(a.astype(BF16), b.astype(BF16), (((2,), (1,)), ((0,), (0,))),
                           preferred_element_type=F32)


def _bmm_nt(a, b):
    return lax.dot_general(a.astype(BF16), b.astype(BF16), (((2,), (2,)), ((0,), (0,))),
                           preferred_element_type=F32)


def _mm_hi(a, b):
    return jnp.dot(a, b, preferred_element_type=F32, precision=HIGHEST)


def _mm_nt_hi(a, b):
    return lax.dot_general(a, b, (((1,), (1,)), ((), ())),
                           preferred_element_type=F32, precision=HIGHEST)


def _rms_rows(x, gain):
    ms = jnp.mean(x * x, axis=-1, keepdims=True)
    return x * lax.rsqrt(ms + EPS) * gain


def _causal_conv(ext, w, width):
    out = ext[HALO:] * w[width - 1:width]
    for k in range(width - 1):
        shift = width - 1 - k
        out = out + pltpu.roll(ext, shift, 0)[HALO:] * w[k:k + 1]
    return out


def _norm_mm_kernel(x_ref, g_ref, w_ref, *rest, has_small):
    if has_small:
        ws_ref, wst_ref, o_ref, os_ref, ost_ref, h_ref = rest
    else:
        o_ref, h_ref = rest

    @pl.when(pl.program_id(1) == 0)
    def _():
        h = _rms_rows(x_ref[...], g_ref[...]).astype(BF16)
        h_ref[...] = h
        if has_small:
            os_ref[...] = jnp.dot(h, ws_ref[...], preferred_element_type=F32)
            ost_ref[...] = _mm_nt(wst_ref[...], h)

    o_ref[...] = jnp.dot(h_ref[...], w_ref[...], preferred_element_type=F32)


def _norm_matmul(x, gain, w, w_small=None, w_small_t=None, *, tm=1024, tn=512):
    m, d = x.shape
    n = w.shape[1]
    has_small = w_small is not None
    in_specs = [pl.BlockSpec((tm, d), lambda i, j: (i, 0)),
                pl.BlockSpec((1, d), lambda i, j: (0, 0)),
                pl.BlockSpec((d, tn), lambda i, j: (0, j))]
    out_specs = [pl.BlockSpec((tm, tn), lambda i, j: (i, j))]
    out_shape = [jax.ShapeDtypeStruct((m, n), F32)]
    args = [x, gain.reshape(1, d), w]
    if has_small:
        ns = w_small.shape[1]
        nst = w_small_t.shape[0]
        in_specs.append(pl.BlockSpec((d, ns), lambda i, j: (0, 0)))
        in_specs.append(pl.BlockSpec((nst, d), lambda i, j: (0, 0)))
        out_specs.append(pl.BlockSpec((tm, ns), lambda i, j: (i, 0)))
        out_specs.append(pl.BlockSpec((nst, tm), lambda i, j: (0, i)))
        out_shape.append(jax.ShapeDtypeStruct((m, ns), F32))
        out_shape.append(jax.ShapeDtypeStruct((nst, m), F32))
        args += [w_small, w_small_t]
    outs = pl.pallas_call(
        functools.partial(_norm_mm_kernel, has_small=has_small),
        grid=(m // tm, n // tn),
        in_specs=in_specs, out_specs=out_specs, out_shape=out_shape,
        scratch_shapes=[pltpu.VMEM((tm, d), BF16)],
        compiler_params=_params("parallel", "arbitrary"),
        name="norm_matmul",
    )(*args)
    return outs if has_small else outs[0]


def _mm_res_kernel(*refs, n_lhs):
    res_ref = refs[0]
    a_refs = refs[1:1 + n_lhs]
    w_refs = refs[1 + n_lhs:1 + 2 * n_lhs]
    o_ref = refs[1 + 2 * n_lhs]
    acc = res_ref[...]
    for a_ref, w_ref in zip(a_refs, w_refs):
        acc = acc + jnp.dot(a_ref[...], w_ref[...], preferred_element_type=F32)
    o_ref[...] = acc


def _matmul_residual(res, lhs_list, w_list, *, tm=512):
    m, n = res.shape
    n_lhs = len(lhs_list)
    in_specs = [pl.BlockSpec((tm, n), lambda i: (i, 0))]
    in_specs += [pl.BlockSpec((tm, a.shape[1]), lambda i: (i, 0)) for a in lhs_list]
    in_specs += [pl.BlockSpec(w.shape, lambda i: (0, 0)) for w in w_list]
    return pl.pallas_call(
        functools.partial(_mm_res_kernel, n_lhs=n_lhs),
        grid=(m // tm,),
        in_specs=in_specs,
        out_specs=pl.BlockSpec((tm, n), lambda i: (i, 0)),
        out_shape=jax.ShapeDtypeStruct((m, n), F32),
        compiler_params=_params("parallel"),
        name="matmul_residual",
    )(res, *lhs_list, *w_list)


def _inv_unit_lower(low, ii, jj):
    eye = (ii == jj).astype(F32)
    same_block = (ii // 16) == (jj // 16)
    d1 = jnp.where(same_block, low, 0.0)
    e = jnp.where(same_block, 0.0, low)
    d2 = _bmm(d1, d1)
    d4 = _bmm(d2, d2)
    d8 = _bmm(d4, d4)
    dinv = _bmm(eye - d1, eye + d2)
    dinv = _bmm(dinv, eye + d4)
    dinv = _bmm(dinv, eye + d8)
    n1 = _bmm(dinv, e)
    n2 = _bmm(n1, n1)
    return _bmm(_bmm(eye - n1, eye + n2), dinv)


def _chunk_cumsum(x, axis):
    pos = lax.broadcasted_iota(jnp.int32, x.shape, axis) % GDN_CHUNK
    s = 1
    while s < GDN_CHUNK:
        x = x + jnp.where(pos >= s, pltpu.roll(x, s, axis), 0.0)
        s *= 2
    return x


def _gdn_kernel(qkv_ref, z_ref, s_ref, st_ref, cw_ref, alog_ref, dt_ref, alog_t_ref, dt_t_ref,
                gain_ref, o_ref, state_ref, halo_ref, *, tc):
    c = GDN_CHUNK
    hd = HEAD_DIM
    width = N_HEADS * hd

    @pl.when(pl.program_id(1) == 0)
    def _():
        state_ref[...] = jnp.zeros_like(state_ref)
        halo_ref[...] = jnp.zeros_like(halo_ref)

    ii = lax.broadcasted_iota(jnp.int32, (c, c), 0)
    jj = lax.broadcasted_iota(jnp.int32, (c, c), 1)
    causal = ii >= jj
    strict = ii > jj
    gain = gain_ref[...]
    cw = cw_ref[...]

    small = s_ref[...]
    beta_all = _sigmoid(small)
    gc_all = _chunk_cumsum(-jnp.exp(alog_ref[...]) * _softplus(small + dt_ref[...]), 0)
    gc_t = _chunk_cumsum(-jnp.exp(alog_t_ref[...]) * _softplus(st_ref[...] + dt_t_ref[...]), 1)

    qs, ks, kbs, decays, rhss, qds, kds, carries = [], [], [], [], [], [], [], []
    for ci in range(tc // c):
        r0 = ci * c
        prev = halo_ref[...] if ci == 0 else qkv_ref[r0 - HALO:r0, :]
        ext = jnp.concatenate([prev, qkv_ref[r0:r0 + c, :]], axis=0)
        qkv = _silu(_causal_conv(ext, cw, GDN_CONV))
        for h in range(N_HEADS):
            q = qkv[:, h * hd:(h + 1) * hd]
            k = qkv[:, width + h * hd:width + (h + 1) * hd]
            v = qkv[:, 2 * width + h * hd:2 * width + (h + 1) * hd]
            q = q * lax.rsqrt(jnp.sum(q * q, axis=-1, keepdims=True) + EPS) * (hd ** -0.5)
            k = k * lax.rsqrt(jnp.sum(k * k, axis=-1, keepdims=True) + EPS)
            beta = beta_all[r0:r0 + c, h:h + 1]
            gc_col = gc_all[r0:r0 + c, N_HEADS + h:N_HEADS + h + 1]
            gc_row = gc_t[N_HEADS + h:N_HEADS + h + 1, r0:r0 + c]
            exp_gc = jnp.exp(gc_col)
            gc_last = gc_col[c - 1:c, :]
            k_beta = k * beta
            qs.append(q)
            ks.append(k)
            kbs.append(k_beta)
            decays.append(jnp.where(causal, jnp.exp(jnp.where(causal, gc_col - gc_row, 0.0)), 0.0))
            rhss.append(jnp.concatenate([v * beta, k_beta * exp_gc], axis=1))
            qds.append(q * exp_gc)
            kds.append(k * jnp.exp(gc_last - gc_col))
            carries.append(jnp.exp(gc_last))
    q_all, k_all, decay = jnp.stack(qs), jnp.stack(ks), jnp.stack(decays)
    low = jnp.where(strict, _bmm_nt(jnp.stack(kbs), k_all) * decay, 0.0)
    uw = _bmm(_inv_unit_lower(low, ii, jj), jnp.stack(rhss))
    attn = _bmm_nt(q_all, k_all) * decay

    state = state_ref[...]
    for ci in range(tc // c):
        r0 = ci * c
        sl = slice(ci * N_HEADS, (ci + 1) * N_HEADS)
        wq = _bmm(jnp.concatenate([uw[sl, :, hd:], jnp.stack(qds[sl])], axis=1), state)
        v_new = uw[sl, :, :hd] - wq[:, :c]
        o = wq[:, c:] + _bmm(attn[sl], v_new)
        state = jnp.stack([state[h] * carries[ci * N_HEADS + h]
                           + _mm_tn(kds[ci * N_HEADS + h], v_new[h]) for h in range(N_HEADS)])
        for h in range(N_HEADS):
            zg = _silu(z_ref[r0:r0 + c, h * hd:(h + 1) * hd])
            o_ref[r0:r0 + c, h * hd:(h + 1) * hd] = (_rms_rows(o[h], gain) * zg).astype(BF16)
    state_ref[...] = state
    halo_ref[...] = qkv_ref[tc - HALO:tc, :]


def _gdn(p_main, p_small, p_small_t, conv_w, a_log, dt_bias, gain, *, batch, seq, tc=256):
    m = batch * seq
    nt = seq // tc
    width = N_HEADS * HEAD_DIM
    n_t = p_small_t.shape[0]
    lane_vec = lambda v: jnp.pad(v.astype(F32), (N_HEADS, HEAD_DIM - 2 * N_HEADS)).reshape(1, HEAD_DIM)
    row_vec = lambda v: jnp.pad(v.astype(F32), (N_HEADS, n_t - 2 * N_HEADS)).reshape(n_t, 1)
    row = lambda b, t: (b * nt + t, 0)
    const = lambda b, t: (0, 0)
    return pl.pallas_call(
        functools.partial(_gdn_kernel, tc=tc),
        grid=(batch, nt),
        in_specs=[pl.BlockSpec((tc, 3 * width), row),
                  pl.BlockSpec((tc, width), lambda b, t: (b * nt + t, 3)),
                  pl.BlockSpec((tc, HEAD_DIM), row),
                  pl.BlockSpec((n_t, tc), lambda b, t: (0, b * nt + t)),
                  pl.BlockSpec((GDN_CONV, 3 * width), const),
                  pl.BlockSpec((1, HEAD_DIM), const),
                  pl.BlockSpec((1, HEAD_DIM), const),
                  pl.BlockSpec((n_t, 1), const),
                  pl.BlockSpec((n_t, 1), const),
                  pl.BlockSpec((1, HEAD_DIM), const)],
        out_specs=pl.BlockSpec((tc, width), row),
        out_shape=jax.ShapeDtypeStruct((m, width), BF16),
        scratch_shapes=[pltpu.VMEM((N_HEADS, HEAD_DIM, HEAD_DIM), F32),
                        pltpu.VMEM((HALO, 3 * width), F32)],
        compiler_params=_params("parallel", "arbitrary"),
        name="gdn",
    )(p_main, p_main, p_small, p_small_t, conv_w, lane_vec(a_log), lane_vec(dt_bias),
      row_vec(a_log), row_vec(dt_bias), gain)


def _hgrn_kernel(q_ref, f_ref, i_ref, g_ref, lbp_ref, gain_ref, o_ref, state_ref,
                 *, tc, layer, sup):
    c = HGRN_CHUNK
    hd = HEAD_DIM
    n_sub = sup // c

    @pl.when(pl.program_id(2) == 0)
    def _():
        state_ref[...] = jnp.zeros_like(state_ref)

    lbp = lbp_ref[...]
    lbe = jnp.exp(lbp - jnp.max(lbp, axis=0, keepdims=True))
    lbs = lbe / jnp.sum(lbe, axis=0, keepdims=True)
    lb = jnp.sum(lbs[0:layer + 1], axis=0, keepdims=True) - lbs[0:1]
    gain = gain_ref[...]

    ri = lax.broadcasted_iota(jnp.int32, (sup, sup), 0)
    rj = lax.broadcasted_iota(jnp.int32, (sup, sup), 1)
    same = (ri // c) == (rj // c)
    cum_mat = (same & (ri >= rj)).astype(F32)
    tot_mat = same.astype(F32)
    sub_row = lax.broadcasted_iota(jnp.int32, (c, hd), 0)
    gi = lax.broadcasted_iota(jnp.int32, (c, c * c), 0)
    gj = lax.broadcasted_iota(jnp.int32, (c, c * c), 1)
    group = ((gj // c) == gi).astype(BF16)
    ones = jnp.ones((hd, hd), BF16)

    def sup_body(si, carry):
        r0 = pl.multiple_of(si * sup, sup)
        f = lb + (1.0 - lb) * _sigmoid(f_ref[pl.ds(r0, sup), :])
        log_f = jnp.log(jnp.maximum(f, F_FLOOR))
        kk = 1.0 - f
        q = _silu(q_ref[pl.ds(r0, sup), :])
        v = i_ref[pl.ds(r0, sup), :]
        b = _mm_hi(cum_mat, log_f)
        b_tot = _mm_hi(tot_mat, log_f)
        e_b = jnp.exp(b)
        q_dec = q * e_b
        k_dec = kk * jnp.exp(b_tot - b)
        outs = []
        for s in range(n_sub):
            lo = s * c
            b_c = b[lo:lo + c]
            kk_c = kk[lo:lo + c]
            v_c = v[lo:lo + c]
            pairs = []
            for i in range(c):
                diff = jnp.where(sub_row <= i, b_c[i:i + 1] - b_c, -1e30)
                pairs.append(jnp.exp(diff) * kk_c * q[lo + i:lo + i + 1])
            pair = jnp.concatenate(pairs, axis=0)
            scores = _mm(pair, ones)
            v_rep = jnp.concatenate([v_c] * c, axis=0)
            o_intra = _mm(group, scores * v_rep)
            st = state_ref[...]
            o_inter = _mm_nt(q_dec[lo:lo + c], st)
            state_ref[...] = st * e_b[lo + c - 1:lo + c] + _mm_tn(v_c, k_dec[lo:lo + c])
            outs.append(o_intra + o_inter)
        o = jnp.concatenate(outs, axis=0)
        gate = _silu(g_ref[pl.ds(r0, sup), :])
        o_ref[pl.ds(r0, sup), :] = (_rms_rows(o, gain) * gate).astype(BF16)
        return carry

    lax.fori_loop(0, tc // sup, sup_body, 0)


def _hgrn(p_main, lower_bounds_raw, gain, *, layer, batch, seq, tc=512, sup=128):
    m = batch * seq
    nt = seq // tc
    n_layers = lower_bounds_raw.shape[0]
    col = lambda base: (lambda b, h, t: (b * nt + t, base + h))
    return pl.pallas_call(
        functools.partial(_hgrn_kernel, tc=tc, layer=layer, sup=sup),
        grid=(batch, N_HEADS, nt),
        in_specs=[pl.BlockSpec((tc, HEAD_DIM), col(16)),
                  pl.BlockSpec((tc, HEAD_DIM), col(20)),
                  pl.BlockSpec((tc, HEAD_DIM), col(24)),
                  pl.BlockSpec((tc, HEAD_DIM), col(28)),
                  pl.BlockSpec((n_layers, HEAD_DIM), lambda b, h, t: (0, h)),
                  pl.BlockSpec((1, HEAD_DIM), lambda b, h, t: (0, 0))],
        out_specs=pl.BlockSpec((tc, HEAD_DIM), col(0)),
        out_shape=jax.ShapeDtypeStruct((m, N_HEADS * HEAD_DIM), BF16),
        scratch_shapes=[pltpu.VMEM((HEAD_DIM, HEAD_DIM), F32)],
        compiler_params=_params("parallel", "parallel", "arbitrary"),
        name="hgrn",
    )(p_main, p_main, p_main, p_main, lower_bounds_raw, gain)


def _rglru_kernel(y_ref, x_ref, cw_ref, cb_ref, wa_ref, ba_ref, wx_ref, bx_ref, lam_ref,
                  o_ref, h_ref, halo_ref, *, tc):
    n_heads = wa_ref.shape[0]
    blk = LRU_BLOCK

    @pl.when(pl.program_id(1) == 0)
    def _():
        h_ref[...] = jnp.zeros_like(h_ref)
        halo_ref[...] = jnp.zeros_like(halo_ref)

    x = x_ref[...]
    ext = jnp.concatenate([halo_ref[...], x], axis=0)
    halo_ref[...] = x[tc - HALO:tc]
    xc = _causal_conv(ext, cw_ref[...], LRU_CONV) + cb_ref[...]
    ra = jnp.concatenate([_mm(xc[:, h * blk:(h + 1) * blk], wa_ref[h]) for h in range(n_heads)],
                         axis=1)
    rx = jnp.concatenate([_mm(xc[:, h * blk:(h + 1) * blk], wx_ref[h]) for h in range(n_heads)],
                         axis=1)
    r = _sigmoid(ra + ba_ref[...])
    gi = _sigmoid(rx + bx_ref[...])
    log_a = -RG_C * r * _softplus(-lam_ref[...])
    a = jnp.exp(log_a)
    th = jnp.tanh(log_a)
    one_minus_a2 = -2.0 * th / (1.0 - th)
    u = jnp.sqrt(jnp.maximum(one_minus_a2, 0.0)) * (gi * xc)

    sub = lax.broadcasted_iota(jnp.int32, a.shape, 0) % HALO
    for s in (1, 2, 4):
        a_prev = jnp.where(sub >= s, pltpu.roll(a, s, 0), 1.0)
        u_prev = jnp.where(sub >= s, pltpu.roll(u, s, 0), 0.0)
        u = a * u_prev + u
        a = a * a_prev
    carry = h_ref[0:1, :]
    tiles = []
    for t in range(tc // HALO):
        ht = a[t * HALO:(t + 1) * HALO] * carry + u[t * HALO:(t + 1) * HALO]
        tiles.append(ht)
        carry = ht[HALO - 1:HALO]
    h_ref[...] = jnp.broadcast_to(carry, h_ref.shape)
    hs = jnp.concatenate(tiles, axis=0)
    y = y_ref[...]
    gate = 0.5 * y * (1.0 + jnp.tanh(0.7978845608028654 * (y + 0.044715 * (y * y * y))))
    o_ref[...] = (hs * gate).astype(BF16)


def _rglru(pc, conv_w, conv_b, wa, ba, wx, bx, lam, *, batch, seq, tc=256):
    m = batch * seq
    nt = seq // tc
    width = conv_w.shape[1]
    vec = pl.BlockSpec((1, width), lambda b, t: (0, 0))
    mat = pl.BlockSpec(wa.shape, lambda b, t: (0, 0, 0))
    return pl.pallas_call(
        functools.partial(_rglru_kernel, tc=tc),
        grid=(batch, nt),
        in_specs=[pl.BlockSpec((tc, width), lambda b, t: (b * nt + t, 0)),
                  pl.BlockSpec((tc, width), lambda b, t: (b * nt + t, 1)),
                  pl.BlockSpec((LRU_CONV, width), lambda b, t: (0, 0)),
                  vec, mat, vec, mat, vec, vec],
        out_specs=pl.BlockSpec((tc, width), lambda b, t: (b * nt + t, 0)),
        out_shape=jax.ShapeDtypeStruct((m, width), BF16),
        scratch_shapes=[pltpu.VMEM((HALO, width), F32), pltpu.VMEM((HALO, width), F32)],
        compiler_params=_params("parallel", "arbitrary"),
        name="rglru",
    )(pc, pc, conv_w, conv_b.reshape(1, width), wa, ba.reshape(1, width), wx,
      bx.reshape(1, width), lam.reshape(1, width))


def _ffn_kernel(x_ref, g_ref, wg_ref, wv_ref, cw_ref, cb_ref, wd_ref, gf_ref, o_ref,
                h_ref, act_ref, halo_ref, *, tiles_per_seq, n_chunks, tf, final_norm):
    i = pl.program_id(0)
    j = pl.program_id(1)

    @pl.when(j == 0)
    def _():
        h_ref[...] = _rms_rows(x_ref[...], g_ref[...]).astype(BF16)

    @pl.when((j == 0) & (i % tiles_per_seq == 0))
    def _():
        halo_ref[...] = jnp.zeros_like(halo_ref)

    h = h_ref[...]
    gate = jnp.dot(h, wg_ref[...], preferred_element_type=F32)
    val = jnp.dot(h, wv_ref[...], preferred_element_type=F32)
    ext = jnp.concatenate([halo_ref[j], gate], axis=0)
    halo_ref[j] = gate[gate.shape[0] - HALO:]
    gc = _causal_conv(ext, cw_ref[...], FFN_CONV) + cb_ref[...]
    act_ref[j] = (_silu(gc) * val).astype(BF16)

    @pl.when(j == n_chunks - 1)
    def _():
        acc = x_ref[...]
        for cidx in range(n_chunks):
            acc = acc + jnp.dot(act_ref[cidx], wd_ref[cidx * tf:(cidx + 1) * tf, :],
                                preferred_element_type=F32)
        if final_norm:
            acc = _rms_rows(acc, gf_ref[...])
        o_ref[...] = acc


def _ffn(x, gain, w_up, conv_w, conv_b, w_down, final_gain, *, seq, final_norm, tm=1024, tf=256):
    m, d = x.shape
    d_ff = w_down.shape[0]
    n_chunks = d_ff // tf
    return pl.pallas_call(
        functools.partial(_ffn_kernel, tiles_per_seq=seq // tm, n_chunks=n_chunks, tf=tf,
                          final_norm=final_norm),
        grid=(m // tm, n_chunks),
        in_specs=[pl.BlockSpec((tm, d), lambda i, j: (i, 0)),
                  pl.BlockSpec((1, d), lambda i, j: (0, 0)),
                  pl.BlockSpec((d, tf), lambda i, j: (0, j)),
                  pl.BlockSpec((d, tf), lambda i, j: (0, n_chunks + j)),
                  pl.BlockSpec((FFN_CONV, tf), lambda i, j: (0, j)),
                  pl.BlockSpec((1, tf), lambda i, j: (0, j)),
                  pl.BlockSpec((d_ff, d), lambda i, j: (0, 0)),
                  pl.BlockSpec((1, d), lambda i, j: (0, 0))],
        out_specs=pl.BlockSpec((tm, d), lambda i, j: (i, 0)),
        out_shape=jax.ShapeDtypeStruct((m, d), F32),
        scratch_shapes=[pltpu.VMEM((tm, d), BF16),
                        pltpu.VMEM((n_chunks, tm, tf), BF16),
                        pltpu.VMEM((n_chunks, HALO, tf), F32)],
        compiler_params=_params("arbitrary", "arbitrary"),
        name="conv_ffn",
    )(x, gain.reshape(1, d), w_up, w_up, conv_w, conv_b.reshape(1, d_ff), w_down,
      final_gain.reshape(1, d))


def _even_layer(x, gain, w_in, conv_w, a_log, dt_bias, gdn_gain, lower_bounds_raw, hgrn_gain,
                w_out, *, layer, batch, seq):
    width = N_HEADS * HEAD_DIM
    split = 4 * width
    w_main = jnp.concatenate([w_in[:, :split], w_in[:, split + 2 * N_HEADS:]], axis=1).astype(BF16)
    w_gates = w_in[:, split:split + 2 * N_HEADS]
    w_small = jnp.pad(w_gates, ((0, 0), (0, HEAD_DIM - 2 * N_HEADS))).astype(BF16)
    w_small_t = jnp.pad(w_gates.T, ((0, 2 * HALO - 2 * N_HEADS), (0, 0))).astype(BF16)
    p_main, p_small, p_small_t = _norm_matmul(x, gain, w_main, w_small, w_small_t)
    o_a = _gdn(p_main, p_small, p_small_t, conv_w, a_log, dt_bias, gdn_gain.reshape(1, HEAD_DIM),
               batch=batch, seq=seq)
    o_b = _hgrn(p_main, lower_bounds_raw, hgrn_gain.reshape(1, HEAD_DIM), layer=layer,
                batch=batch, seq=seq)
    w_out = w_out.astype(BF16)
    return _matmul_residual(x, [o_a, o_b], [w_out[:width], w_out[width:]])


def _odd_layer(x, gain, w_in, conv_w, conv_b, wa, ba, wx, bx, lam, w_out, *, batch, seq):
    pc = _norm_matmul(x, gain, w_in.astype(BF16))
    hs = _rglru(pc, conv_w, conv_b, wa.astype(BF16), ba, wx.astype(BF16), bx, lam,
                batch=batch, seq=seq)
    return _matmul_residual(x, [hs], [w_out.astype(BF16)])


def kernel(x, norm_mix, norm_ffn, norm_final, ab_w_in, gdn_conv_w, gdn_a_log, gdn_dt_bias,
           gdn_norm, hgrn_lower_bounds, hgrn_norm, ab_w_out, c_w_in, c_conv_w, c_conv_b,
           c_gate_a_w, c_gate_a_b, c_gate_x_w, c_gate_x_b, c_lambda, c_w_out,
           ffn_w_up, ffn_conv_w, ffn_conv_b, ffn_w_down):
    batch, seq, d = x.shape
    depth = norm_mix.shape[0]
    x = x.reshape(batch * seq, d)
    for layer in range(depth):
        j = layer // 2
        if layer % 2 == 0:
            x = _even_layer(x, norm_mix[layer], ab_w_in[j], gdn_conv_w[j], gdn_a_log[j],
                            gdn_dt_bias[j], gdn_norm[j], hgrn_lower_bounds, hgrn_norm[j],
                            ab_w_out[j], layer=j, batch=batch, seq=seq)
        else:
            x = _odd_layer(x, norm_mix[layer], c_w_in[j], c_conv_w[j], c_conv_b[j],
                           c_gate_a_w[j], c_gate_a_b[j], c_gate_x_w[j], c_gate_x_b[j],
                           c_lambda[j], c_w_out[j], batch=batch, seq=seq)
        x = _ffn(x, norm_ffn[layer], ffn_w_up[layer].astype(BF16), ffn_conv_w[layer],
                 ffn_conv_b[layer], ffn_w_down[layer].astype(BF16), norm_final, seq=seq,
                 final_norm=layer == depth - 1)
    return x.reshape(batch, seq, d)
```

```python
import functools

import jax
import jax.numpy as jnp
from jax import lax
from jax.experimental import pallas as pl
from jax.experimental.pallas import tpu as pltpu

F32 = jnp.float32
BF16 = jnp.bfloat16
HIGHEST = lax.Precision.HIGHEST

EPS = 1e-6
F_FLOOR = 1e-30
RG_C = 8.0

HEAD_DIM = 128
N_HEADS = 4
GDN_CHUNK = 64
GDN_CONV = 4
HGRN_CHUNK = 16
LRU_BLOCK = 256
LRU_CONV = 4
FFN_CONV = 3
HALO = 8

VMEM_LIMIT_BYTES = 56 * 1024 * 1024


def _params(*semantics):
    return pltpu.CompilerParams(dimension_semantics=semantics,
                                vmem_limit_bytes=VMEM_LIMIT_BYTES)


def _sigmoid(x):
    return 1.0 / (1.0 + jnp.exp(-x))


def _silu(x):
    return x * _sigmoid(x)


def _softplus(x):
    return jnp.maximum(x, 0.0) + jnp.log(1.0 + jnp.exp(-jnp.abs(x)))


def _mm(a, b):
    return jnp.dot(a.astype(BF16), b.astype(BF16), preferred_element_type=F32)


def _mm_nt(a, b):
    return lax.dot_general(a.astype(BF16), b.astype(BF16), (((1,), (1,)), ((), ())),
                           preferred_element_type=F32)


def _mm_tn(a, b):
    return lax.dot_general(a.astype(BF16), b.astype(BF16), (((0,), (0,)), ((), ())),
                           preferred_element_type=F32)


def _bmm(a, b):
    return lax.dot_general(a.astype(BF16), b.astype(BF16), (((2,), (1,)), ((0,), (0,))),
                           preferred_element_type=F32)


def _bmm_nt(a, b):
    return lax.dot_general(a.astype(BF16), b.astype(BF16), (((2,), (2,)), ((0,), (0,))),
                           preferred_element_type=F32)


def _mm_hi(a, b):
    return jnp.dot(a, b, preferred_element_type=F32, precision=HIGHEST)


def _mm_nt_hi(a, b):
    return lax.dot_general(a, b, (((1,), (1,)), ((), ())),
                           preferred_element_type=F32, precision=HIGHEST)


def _rms_rows(x, gain):
    ms = jnp.mean(x * x, axis=-1, keepdims=True)
    return x * lax.rsqrt(ms + EPS) * gain


def _causal_conv(ext, w, width):
    out = ext[HALO:] * w[width - 1:width]
    for k in range(width - 1):
        shift = width - 1 - k
        out = out + pltpu.roll(ext, shift, 0)[HALO:] * w[k:k + 1]
    return out


def _norm_mm_kernel(x_ref, g_ref, w_ref, *rest, has_small):
    if has_small:
        ws_ref, wst_ref, o_ref, os_ref, ost_ref, h_ref = rest
    else:
        o_ref, h_ref = rest

    @pl.when(pl.program_id(1) == 0)
    def _():
        h = _rms_rows(x_ref[...], g_ref[...]).astype(BF16)
        h_ref[...] = h
        if has_small:
            os_ref[...] = jnp.dot(h, ws_ref[...], preferred_element_type=F32)
            ost_ref[...] = _mm_nt(wst_ref[...], h)

    o_ref[...] = jnp.dot(h_ref[...], w_ref[...], preferred_element_type=F32)


def _norm_matmul(x, gain, w, w_small=None, w_small_t=None, *, tm=1024, tn=1024):
    m, d = x.shape
    n = w.shape[1]
    has_small = w_small is not None
    in_specs = [pl.BlockSpec((tm, d), lambda i, j: (i, 0)),
                pl.BlockSpec((1, d), lambda i, j: (0, 0)),
                pl.BlockSpec((d, tn), lambda i, j: (0, j))]
    out_specs = [pl.BlockSpec((tm, tn), lambda i, j: (i, j))]
    out_shape = [jax.ShapeDtypeStruct((m, n), F32)]
    args = [x, gain.reshape(1, d), w]
    if has_small:
        ns = w_small.shape[1]
        nst = w_small_t.shape[0]
        in_specs.append(pl.BlockSpec((d, ns), lambda i, j: (0, 0)))
        in_specs.append(pl.BlockSpec((nst, d), lambda i, j: (0, 0)))
        out_specs.append(pl.BlockSpec((tm, ns), lambda i, j: (i, 0)))
        out_specs.append(pl.BlockSpec((nst, tm), lambda i, j: (0, i)))
        out_shape.append(jax.ShapeDtypeStruct((m, ns), F32))
        out_shape.append(jax.ShapeDtypeStruct((nst, m), F32))
        args += [w_small, w_small_t]
    outs = pl.pallas_call(
        functools.partial(_norm_mm_kernel, has_small=has_small),
        grid=(m // tm, n // tn),
        in_specs=in_specs, out_specs=out_specs, out_shape=out_shape,
        scratch_shapes=[pltpu.VMEM((tm, d), BF16)],
        compiler_params=_params("parallel", "arbitrary"),
        name="norm_matmul",
    )(*args)
    return outs if has_small else outs[0]


def _inv_unit_lower(low, ii, jj):
    eye = (ii == jj).astype(F32)
    same_block = (ii // 16) == (jj // 16)
    d1 = jnp.where(same_block, low, 0.0)
    e = jnp.where(same_block, 0.0, low)
    d2 = _bmm(d1, d1)
    d4 = _bmm(d2, d2)
    d8 = _bmm(d4, d4)
    dinv = _bmm(eye - d1, eye + d2)
    dinv = _bmm(dinv, eye + d4)
    dinv = _bmm(dinv, eye + d8)
    n1 = _bmm(dinv, e)
    n2 = _bmm(n1, n1)
    return _bmm(_bmm(eye - n1, eye + n2), dinv)


def _chunk_cumsum(x, axis):
    pos = lax.broadcasted_iota(jnp.int32, x.shape, axis) % GDN_CHUNK
    s = 1
    while s < GDN_CHUNK:
        x = x + jnp.where(pos >= s, pltpu.roll(x, s, axis), 0.0)
        s *= 2
    return x


def _gdn_kernel(qkv_ref, z_ref, s_ref, st_ref, cw_ref, alog_ref, dt_ref, alog_t_ref, dt_t_ref,
                gain_ref, o_ref, state_ref, halo_ref, *, tc):
    c = GDN_CHUNK
    hd = HEAD_DIM
    width = N_HEADS * hd

    @pl.when(pl.program_id(1) == 0)
    def _():
        state_ref[...] = jnp.zeros_like(state_ref)
        halo_ref[...] = jnp.zeros_like(halo_ref)

    ii = lax.broadcasted_iota(jnp.int32, (c, c), 0)
    jj = lax.broadcasted_iota(jnp.int32, (c, c), 1)
    causal = ii >= jj
    strict = ii > jj
    gain = gain_ref[...]
    cw = cw_ref[...]

    small = s_ref[...]
    beta_all = _sigmoid(small)
    gc_all = _chunk_cumsum(-jnp.exp(alog_ref[...]) * _softplus(small + dt_ref[...]), 0)
    gc_t = _chunk_cumsum(-jnp.exp(alog_t_ref[...]) * _softplus(st_ref[...] + dt_t_ref[...]), 1)

    qs, ks, kbs, decays, rhss, qds, kds, carries = [], [], [], [], [], [], [], []
    for ci in range(tc // c):
        r0 = ci * c
        prev = halo_ref[...] if ci == 0 else qkv_ref[r0 - HALO:r0, :]
        ext = jnp.concatenate([prev, qkv_ref[r0:r0 + c, :]], axis=0)
        qkv = _silu(_causal_conv(ext, cw, GDN_CONV))
        for h in range(N_HEADS):
            q = qkv[:, h * hd:(h + 1) * hd]
            k = qkv[:, width + h * hd:width + (h + 1) * hd]
            v = qkv[:, 2 * width + h * hd:2 * width + (h + 1) * hd]
            q = q * lax.rsqrt(jnp.sum(q * q, axis=-1, keepdims=True) + EPS) * (hd ** -0.5)
            k = k * lax.rsqrt(jnp.sum(k * k, axis=-1, keepdims=True) + EPS)
            beta = beta_all[r0:r0 + c, h:h + 1]
            gc_col = gc_all[r0:r0 + c, N_HEADS + h:N_HEADS + h + 1]
            gc_row = gc_t[N_HEADS + h:N_HEADS + h + 1, r0:r0 + c]
            exp_gc = jnp.exp(gc_col)
            gc_last = gc_col[c - 1:c, :]
            k_beta = k * beta
            qs.append(q)
            ks.append(k)
            kbs.append(k_beta)
            decays.append(jnp.where(causal, jnp.exp(jnp.where(causal, gc_col - gc_row, 0.0)), 0.0))
            rhss.append(jnp.concatenate([v * beta, k_beta * exp_gc], axis=1))
            qds.append(q * exp_gc)
            kds.append(k * jnp.exp(gc_last - gc_col))
            carries.append(jnp.exp(gc_last))
    q_all, k_all, decay = jnp.stack(qs), jnp.stack(ks), jnp.stack(decays)
    low = jnp.where(strict, _bmm_nt(jnp.stack(kbs), k_all) * decay, 0.0)
    uw = _bmm(_inv_unit_lower(low, ii, jj), jnp.stack(rhss))
    attn = _bmm_nt(q_all, k_all) * decay

    state = state_ref[...]
    for ci in range(tc // c):
        r0 = ci * c
        sl = slice(ci * N_HEADS, (ci + 1) * N_HEADS)
        wq = _bmm(jnp.concatenate([uw[sl, :, hd:], jnp.stack(qds[sl])], axis=1), state)
        v_new = uw[sl, :, :hd] - wq[:, :c]
        o = wq[:, c:] + _bmm(attn[sl], v_new)
        state = jnp.stack([state[h] * carries[ci * N_HEADS + h]
                           + _mm_tn(kds[ci * N_HEADS + h], v_new[h]) for h in range(N_HEADS)])
        for h in range(N_HEADS):
            zg = _silu(z_ref[r0:r0 + c, h * hd:(h + 1) * hd])
            o_ref[r0:r0 + c, h * hd:(h + 1) * hd] = (_rms_rows(o[h], gain) * zg).astype(BF16)
    state_ref[...] = state
    halo_ref[...] = qkv_ref[tc - HALO:tc, :]


def _gdn(p_main, p_small, p_small_t, conv_w, a_log, dt_bias, gain, *, batch, seq, tc=256):
    m = batch * seq
    nt = seq // tc
    width = N_HEADS * HEAD_DIM
    n_t = p_small_t.shape[0]
    lane_vec = lambda v: jnp.pad(v.astype(F32), (N_HEADS, HEAD_DIM - 2 * N_HEADS)).reshape(1, HEAD_DIM)
    row_vec = lambda v: jnp.pad(v.astype(F32), (N_HEADS, n_t - 2 * N_HEADS)).reshape(n_t, 1)
    row = lambda b, t: (b * nt + t, 0)
    const = lambda b, t: (0, 0)
    return pl.pallas_call(
        functools.partial(_gdn_kernel, tc=tc),
        grid=(batch, nt),
        in_specs=[pl.BlockSpec((tc, 3 * width), row),
                  pl.BlockSpec((tc, width), lambda b, t: (b * nt + t, 3)),
                  pl.BlockSpec((tc, HEAD_DIM), row),
                  pl.BlockSpec((n_t, tc), lambda b, t: (0, b * nt + t)),
                  pl.BlockSpec((GDN_CONV, 3 * width), const),
                  pl.BlockSpec((1, HEAD_DIM), const),
                  pl.BlockSpec((1, HEAD_DIM), const),
                  pl.BlockSpec((n_t, 1), const),
                  pl.BlockSpec((n_t, 1), const),
                  pl.BlockSpec((1, HEAD_DIM), const)],
        out_specs=pl.BlockSpec((tc, width), row),
        out_shape=jax.ShapeDtypeStruct((m, width), BF16),
        scratch_shapes=[pltpu.VMEM((N_HEADS, HEAD_DIM, HEAD_DIM), F32),
                        pltpu.VMEM((HALO, 3 * width), F32)],
        compiler_params=_params("parallel", "arbitrary"),
        name="gdn",
    )(p_main, p_main, p_small, p_small_t, conv_w, lane_vec(a_log), lane_vec(dt_bias),
      row_vec(a_log), row_vec(dt_bias), gain)


def _hgrn_kernel(q_ref, f_ref, i_ref, g_ref, lbp_ref, gain_ref, o_ref, state_ref,
                 *, tc, layer, sup):
    c = HGRN_CHUNK
    hd = HEAD_DIM
    n_sub = sup // c

    @pl.when(pl.program_id(2) == 0)
    def _():
        state_ref[...] = jnp.zeros_like(state_ref)

    lbp = lbp_ref[...]
    lbe = jnp.exp(lbp - jnp.max(lbp, axis=0, keepdims=True))
    lbs = lbe / jnp.sum(lbe, axis=0, keepdims=True)
    lb = jnp.sum(lbs[0:layer + 1], axis=0, keepdims=True) - lbs[0:1]
    gain = gain_ref[...]

    half = c // 2
    pos = lax.broadcasted_iota(jnp.int32, (sup, hd), 0) % c
    tile_row = lax.broadcasted_iota(jnp.int32, (half, hd), 0)
    n_pair = half * half + half * c
    gi = lax.broadcasted_iota(jnp.int32, (c, n_pair), 0)
    gj = lax.broadcasted_iota(jnp.int32, (c, n_pair), 1)
    owner = jnp.where(gj < half * half, gj // half, half + (gj - half * half) // c)
    group = jnp.broadcast_to((owner == gi).astype(BF16), (n_sub, c, n_pair))
    ones = jnp.ones((hd, hd), BF16)

    def sup_body(si, carry):
        r0 = pl.multiple_of(si * sup, sup)
        f = lb + (1.0 - lb) * _sigmoid(f_ref[pl.ds(r0, sup), :])
        log_f = jnp.log(jnp.maximum(f, F_FLOOR))
        kk = 1.0 - f
        q = _silu(q_ref[pl.ds(r0, sup), :])
        v = i_ref[pl.ds(r0, sup), :]
        b = log_f
        tail = log_f
        s = 1
        while s < c:
            b = b + jnp.where(pos >= s, pltpu.roll(b, s, 0), 0.0)
            tail = tail + jnp.where(pos < c - s, pltpu.roll(tail, sup - s, 0), 0.0)
            s *= 2
        e_b = jnp.exp(b)
        q_dec = q * e_b
        k_dec = kk * jnp.exp(tail - log_f)
        pairs, v_reps = [], []
        for sc in range(n_sub):
            lo = sc * c
            b_a, b_b = b[lo:lo + half], b[lo + half:lo + c]
            k_a, k_b = kk[lo:lo + half], kk[lo + half:lo + c]
            v_a = v[lo:lo + half]
            v_c = v[lo:lo + c]
            for i in range(c):
                b_i = b[lo + i:lo + i + 1]
                q_i = q[lo + i:lo + i + 1]
                if i < half:
                    pairs.append(jnp.exp(jnp.where(tile_row <= i, b_i - b_a, -1e30)) * k_a * q_i)
                    v_reps.append(v_a)
                else:
                    pairs.append(jnp.exp(b_i - b_a) * k_a * q_i)
                    pairs.append(jnp.exp(jnp.where(tile_row <= i - half, b_i - b_b, -1e30)) * k_b * q_i)
                    v_reps.append(v_c)
        scores = _mm(jnp.concatenate(pairs, axis=0), ones)
        weighted = (scores * jnp.concatenate(v_reps, axis=0)).reshape(n_sub, n_pair, hd)
        o_intra = _bmm(group, weighted)
        updates = [_mm_tn(v[sc * c:(sc + 1) * c], k_dec[sc * c:(sc + 1) * c]) for sc in range(n_sub)]
        states = [carry]
        for sc in range(n_sub):
            states.append(states[sc] * e_b[sc * c + c - 1:sc * c + c] + updates[sc])
        o = jnp.concatenate([o_intra[sc] + _mm_nt(q_dec[sc * c:(sc + 1) * c], states[sc])
                             for sc in range(n_sub)], axis=0)
        st = states[n_sub]
        gate = _silu(g_ref[pl.ds(r0, sup), :])
        o_ref[pl.ds(r0, sup), :] = (_rms_rows(o, gain) * gate).astype(BF16)
        return st

    state_ref[...] = lax.fori_loop(0, tc // sup, sup_body, state_ref[...], unroll=True)


def _hgrn(p_main, lower_bounds_raw, gain, *, layer, batch, seq, tc=512, sup=128):
    m = batch * seq
    nt = seq // tc
    n_layers = lower_bounds_raw.shape[0]
    col = lambda base: (lambda b, h, t: (b * nt + t, base + h))
    return pl.pallas_call(
        functools.partial(_hgrn_kernel, tc=tc, layer=layer, sup=sup),
        grid=(batch, N_HEADS, nt),
        in_specs=[pl.BlockSpec((tc, HEAD_DIM), col(16)),
                  pl.BlockSpec((tc, HEAD_DIM), col(20)),
                  pl.BlockSpec((tc, HEAD_DIM), col(24)),
                  pl.BlockSpec((tc, HEAD_DIM), col(28)),
                  pl.BlockSpec((n_layers, HEAD_DIM), lambda b, h, t: (0, h)),
                  pl.BlockSpec((1, HEAD_DIM), lambda b, h, t: (0, 0))],
        out_specs=pl.BlockSpec((tc, HEAD_DIM), col(0)),
        out_shape=jax.ShapeDtypeStruct((m, N_HEADS * HEAD_DIM), BF16),
        scratch_shapes=[pltpu.VMEM((HEAD_DIM, HEAD_DIM), F32)],
        compiler_params=_params("parallel", "parallel", "arbitrary"),
        name="hgrn",
    )(p_main, p_main, p_main, p_main, lower_bounds_raw, gain)


def _rglru_kernel(y_ref, x_ref, cw_ref, cb_ref, wa_ref, ba_ref, wx_ref, bx_ref, lam_ref,
                  o_ref, h_ref, halo_ref, *, tc):
    n_heads = wa_ref.shape[0]
    blk = LRU_BLOCK

    @pl.when(pl.program_id(1) == 0)
    def _():
        h_ref[...] = jnp.zeros_like(h_ref)
        halo_ref[...] = jnp.zeros_like(halo_ref)

    x = x_ref[...]
    ext = jnp.concatenate([halo_ref[...], x], axis=0)
    halo_ref[...] = x[tc - HALO:tc]
    xc = _causal_conv(ext, cw_ref[...], LRU_CONV) + cb_ref[...]
    ra = jnp.concatenate([_mm(xc[:, h * blk:(h + 1) * blk], wa_ref[h]) for h in range(n_heads)],
                         axis=1)
    rx = jnp.concatenate([_mm(xc[:, h * blk:(h + 1) * blk], wx_ref[h]) for h in range(n_heads)],
                         axis=1)
    r = _sigmoid(ra + ba_ref[...])
    gi = _sigmoid(rx + bx_ref[...])
    log_a = -RG_C * r * _softplus(-lam_ref[...])
    a = jnp.exp(log_a)
    th = jnp.tanh(log_a)
    one_minus_a2 = -2.0 * th / (1.0 - th)
    u = jnp.sqrt(jnp.maximum(one_minus_a2, 0.0)) * (gi * xc)

    sub = lax.broadcasted_iota(jnp.int32, a.shape, 0) % HALO
    for s in (1, 2, 4):
        a_prev = jnp.where(sub >= s, pltpu.roll(a, s, 0), 1.0)
        u_prev = jnp.where(sub >= s, pltpu.roll(u, s, 0), 0.0)
        u = a * u_prev + u
        a = a * a_prev
    carry = h_ref[0:1, :]
    tiles = []
    for t in range(tc // HALO):
        ht = a[t * HALO:(t + 1) * HALO] * carry + u[t * HALO:(t + 1) * HALO]
        tiles.append(ht)
        carry = ht[HALO - 1:HALO]
    h_ref[...] = jnp.broadcast_to(carry, h_ref.shape)
    hs = jnp.concatenate(tiles, axis=0)
    y = y_ref[...]
    gate = 0.5 * y * (1.0 + jnp.tanh(0.7978845608028654 * (y + 0.044715 * (y * y * y))))
    o_ref[...] = (hs * gate).astype(BF16)


def _rglru(pc, conv_w, conv_b, wa, ba, wx, bx, lam, *, batch, seq, tc=256):
    m = batch * seq
    nt = seq // tc
    width = conv_w.shape[1]
    vec = pl.BlockSpec((1, width), lambda b, t: (0, 0))
    mat = pl.BlockSpec(wa.shape, lambda b, t: (0, 0, 0))
    return pl.pallas_call(
        functools.partial(_rglru_kernel, tc=tc),
        grid=(batch, nt),
        in_specs=[pl.BlockSpec((tc, width), lambda b, t: (b * nt + t, 0)),
                  pl.BlockSpec((tc, width), lambda b, t: (b * nt + t, 1)),
                  pl.BlockSpec((LRU_CONV, width), lambda b, t: (0, 0)),
                  vec, mat, vec, mat, vec, vec],
        out_specs=pl.BlockSpec((tc, width), lambda b, t: (b * nt + t, 0)),
        out_shape=jax.ShapeDtypeStruct((m, width), BF16),
        scratch_shapes=[pltpu.VMEM((HALO, width), F32), pltpu.VMEM((HALO, width), F32)],
        compiler_params=_params("parallel", "arbitrary"),
        name="rglru",
    )(pc, pc, conv_w, conv_b.reshape(1, width), wa, ba.reshape(1, width), wx,
      bx.reshape(1, width), lam.reshape(1, width))


def _ffn_kernel(*refs, n_mix, tiles_per_seq, tf, final_norm):
    x_ref = refs[0]
    mix_refs = refs[1:1 + n_mix]
    wmix_refs = refs[1 + n_mix:1 + 2 * n_mix]
    (g_ref, wu_ref, cw_ref, cb_ref, wd_ref, gf_ref, o_ref,
     xm_ref, h_ref, act_ref, halo_ref) = refs[1 + 2 * n_mix:]
    d_ff = wd_ref.shape[0]

    @pl.when(pl.program_id(0) % tiles_per_seq == 0)
    def _():
        halo_ref[...] = jnp.zeros_like(halo_ref)

    xm = x_ref[...]
    for a_ref, w_ref in zip(mix_refs, wmix_refs):
        xm = xm + jnp.dot(a_ref[...], w_ref[...], preferred_element_type=F32)
    xm_ref[...] = xm
    h_ref[...] = _rms_rows(xm, g_ref[...]).astype(BF16)
    for c0 in range(0, d_ff, tf):
        h = h_ref[...]
        gate = jnp.dot(h, wu_ref[:, c0:c0 + tf], preferred_element_type=F32)
        val = jnp.dot(h, wu_ref[:, d_ff + c0:d_ff + c0 + tf], preferred_element_type=F32)
        ext = jnp.concatenate([halo_ref[:, c0:c0 + tf], gate], axis=0)
        halo_ref[:, c0:c0 + tf] = gate[gate.shape[0] - HALO:]
        gc = _causal_conv(ext, cw_ref[:, c0:c0 + tf], FFN_CONV) + cb_ref[:, c0:c0 + tf]
        act_ref[:, c0:c0 + tf] = (_silu(gc) * val).astype(BF16)
    acc = xm_ref[...] + jnp.dot(act_ref[...], wd_ref[...], preferred_element_type=F32)
    if final_norm:
        acc = _rms_rows(acc, gf_ref[...])
    o_ref[...] = acc


def _mix_ffn(x, mix, w_mix, gain, w_up, conv_w, conv_b, w_down, final_gain, *, seq, final_norm,
             tm=1024, tf=256):
    m, d = x.shape
    d_ff = w_down.shape[0]
    row = lambda i: (i, 0)
    const = lambda i: (0, 0)
    resident = dict(pipeline_mode=pl.Buffered(1))
    in_specs = [pl.BlockSpec((tm, d), row)]
    in_specs += [pl.BlockSpec((tm, a.shape[1]), row) for a in mix]
    in_specs += [pl.BlockSpec(w.shape, const, **resident) for w in w_mix]
    in_specs += [pl.BlockSpec((1, d), const),
                 pl.BlockSpec((d, 2 * d_ff), const, **resident),
                 pl.BlockSpec((FFN_CONV, d_ff), const),
                 pl.BlockSpec((1, d_ff), const),
                 pl.BlockSpec((d_ff, d), const, **resident),
                 pl.BlockSpec((1, d), const)]
    return pl.pallas_call(
        functools.partial(_ffn_kernel, n_mix=len(mix), tiles_per_seq=seq // tm, tf=tf,
                          final_norm=final_norm),
        grid=(m // tm,),
        in_specs=in_specs,
        out_specs=pl.BlockSpec((tm, d), row),
        out_shape=jax.ShapeDtypeStruct((m, d), F32),
        scratch_shapes=[pltpu.VMEM((tm, d), F32),
                        pltpu.VMEM((tm, d), BF16),
                        pltpu.VMEM((tm, d_ff), BF16),
                        pltpu.VMEM((HALO, d_ff), F32)],
        compiler_params=_params("arbitrary"),
        name="mix_ffn",
    )(x, *mix, *w_mix, gain.reshape(1, d), w_up, conv_w, conv_b.reshape(1, d_ff), w_down,
      final_gain.reshape(1, d))


def _even_layer(x, gain, w_in, conv_w, a_log, dt_bias, gdn_gain, lower_bounds_raw, hgrn_gain,
                w_out, *, layer, batch, seq):
    width = N_HEADS * HEAD_DIM
    split = 4 * width
    w_main = jnp.concatenate([w_in[:, :split], w_in[:, split + 2 * N_HEADS:]], axis=1).astype(BF16)
    w_gates = w_in[:, split:split + 2 * N_HEADS]
    w_small = jnp.pad(w_gates, ((0, 0), (0, HEAD_DIM - 2 * N_HEADS))).astype(BF16)
    w_small_t = jnp.pad(w_gates.T, ((0, 2 * HALO - 2 * N_HEADS), (0, 0))).astype(BF16)
    p_main, p_small, p_small_t = _norm_matmul(x, gain, w_main, w_small, w_small_t)
    o_a = _gdn(p_main, p_small, p_small_t, conv_w, a_log, dt_bias, gdn_gain.reshape(1, HEAD_DIM),
               batch=batch, seq=seq)
    o_b = _hgrn(p_main, lower_bounds_raw, hgrn_gain.reshape(1, HEAD_DIM), layer=layer,
                batch=batch, seq=seq)
    w_out = w_out.astype(BF16)
    return [o_a, o_b], [w_out[:width], w_out[width:]]


def _odd_layer(x, gain, w_in, conv_w, conv_b, wa, ba, wx, bx, lam, w_out, *, batch, seq):
    pc = _norm_matmul(x, gain, w_in.astype(BF16))
    hs = _rglru(pc, conv_w, conv_b, wa.astype(BF16), ba, wx.astype(BF16), bx, lam,
                batch=batch, seq=seq)
    return [hs], [w_out.astype(BF16)]


def kernel(x, norm_mix, norm_ffn, norm_final, ab_w_in, gdn_conv_w, gdn_a_log, gdn_dt_bias,
           gdn_norm, hgrn_lower_bounds, hgrn_norm, ab_w_out, c_w_in, c_conv_w, c_conv_b,
           c_gate_a_w, c_gate_a_b, c_gate_x_w, c_gate_x_b, c_lambda, c_w_out,
           ffn_w_up, ffn_conv_w, ffn_conv_b, ffn_w_down):
    batch, seq, d = x.shape
    depth = norm_mix.shape[0]
    x = x.reshape(batch * seq, d)
    for layer in range(depth):
        j = layer // 2
        if layer % 2 == 0:
            mix, w_mix = _even_layer(x, norm_mix[layer], ab_w_in[j], gdn_conv_w[j], gdn_a_log[j],
                                     gdn_dt_bias[j], gdn_norm[j], hgrn_lower_bounds,
                                     hgrn_norm[j], ab_w_out[j], layer=j, batch=batch, seq=seq)
        else:
            mix, w_mix = _odd_layer(x, norm_mix[layer], c_w_in[j], c_conv_w[j], c_conv_b[j],
                                    c_gate_a_w[j], c_gate_a_b[j], c_gate_x_w[j], c_gate_x_b[j],
                                    c_lambda[j], c_w_out[j], batch=batch, seq=seq)
        x = _mix_ffn(x, mix, w_mix, norm_ffn[layer], ffn_w_up[layer].astype(BF16),
                     ffn_conv_w[layer], ffn_conv_b[layer], ffn_w_down[layer].astype(BF16),
                     norm_final, seq=seq, final_norm=layer == depth - 1)
    return x.reshape(batch, seq, d)
```

```python
import functools

import jax
import jax.numpy as jnp
from jax import lax
from jax.experimental import pallas as pl
from jax.experimental.pallas import tpu as pltpu

F32 = jnp.float32
BF16 = jnp.bfloat16

EPS = 1e-6
F_FLOOR = 1e-30
RG_C = 8.0

HEAD_DIM = 128
N_HEADS = 4
GDN_CHUNK = 64
GDN_CONV = 4
HGRN_CHUNK = 16
HGRN_SUPER = 128
LRU_BLOCK = 256
LRU_CONV = 4
FFN_CONV = 3
HALO = 8

VMEM_LIMIT_BYTES = 56 * 1024 * 1024
RESIDENT = dict(pipeline_mode=pl.Buffered(1))


def _params(*semantics):
    return pltpu.CompilerParams(dimension_semantics=semantics,
                                vmem_limit_bytes=VMEM_LIMIT_BYTES)


def _sigmoid(x):
    return 1.0 / (1.0 + jnp.exp(-x))


def _silu(x):
    return x * _sigmoid(x)


def _softplus(x):
    return jnp.maximum(x, 0.0) + jnp.log(1.0 + jnp.exp(-jnp.abs(x)))


def _mm(a, b):
    return jnp.dot(a.astype(BF16), b.astype(BF16), preferred_element_type=F32)


def _mm_nt(a, b):
    return lax.dot_general(a.astype(BF16), b.astype(BF16), (((1,), (1,)), ((), ())),
                           preferred_element_type=F32)


def _mm_tn(a, b):
    return lax.dot_general(a.astype(BF16), b.astype(BF16), (((0,), (0,)), ((), ())),
                           preferred_element_type=F32)


def _bmm(a, b):
    return lax.dot_general(a.astype(BF16), b.astype(BF16), (((2,), (1,)), ((0,), (0,))),
                           preferred_element_type=F32)


def _bmm_nt(a, b):
    return lax.dot_general(a.astype(BF16), b.astype(BF16), (((2,), (2,)), ((0,), (0,))),
                           preferred_element_type=F32)


def _rms_rows(x, gain):
    ms = jnp.mean(x * x, axis=-1, keepdims=True)
    return x * lax.rsqrt(ms + EPS) * gain


def _causal_conv(ext, w, width):
    out = ext[HALO:] * w[width - 1:width]
    for k in range(width - 1):
        shift = width - 1 - k
        out = out + pltpu.roll(ext, shift, 0)[HALO:] * w[k:k + 1]
    return out


def _inv_unit_lower(low, ii, jj):
    eye = (ii == jj).astype(F32)
    same_block = (ii // 16) == (jj // 16)
    d1 = jnp.where(same_block, low, 0.0)
    e = jnp.where(same_block, 0.0, low)
    d2 = _bmm(d1, d1)
    d4 = _bmm(d2, d2)
    d8 = _bmm(d4, d4)
    dinv = _bmm(eye - d1, eye + d2)
    dinv = _bmm(dinv, eye + d4)
    dinv = _bmm(dinv, eye + d8)
    n1 = _bmm(dinv, e)
    n2 = _bmm(n1, n1)
    return _bmm(_bmm(eye - n1, eye + n2), dinv)


def _chunk_cumsum(x, axis):
    pos = lax.broadcasted_iota(jnp.int32, x.shape, axis) % GDN_CHUNK
    s = 1
    while s < GDN_CHUNK:
        x = x + jnp.where(pos >= s, pltpu.roll(x, s, axis), 0.0)
        s *= 2
    return x


def _gdn_block(p_ref, small, small_t, cw, alog, dt, alog_t, dt_t, gain, o_ref, state_ref,
               halo_ref, tc):
    c = GDN_CHUNK
    hd = HEAD_DIM
    width = N_HEADS * hd
    ii = lax.broadcasted_iota(jnp.int32, (c, c), 0)
    jj = lax.broadcasted_iota(jnp.int32, (c, c), 1)
    causal = ii >= jj
    strict = ii > jj

    beta_all = _sigmoid(small)
    gc_all = _chunk_cumsum(-jnp.exp(alog) * _softplus(small + dt), 0)
    gc_t = _chunk_cumsum(-jnp.exp(alog_t) * _softplus(small_t + dt_t), 1)

    qs, ks, kbs, decays, rhss, qds, kds, carries = [], [], [], [], [], [], [], []
    for ci in range(tc // c):
        r0 = ci * c
        prev = halo_ref[...] if ci == 0 else p_ref[r0 - HALO:r0, :3 * width]
        ext = jnp.concatenate([prev, p_ref[r0:r0 + c, :3 * width]], axis=0)
        qkv = _silu(_causal_conv(ext, cw, GDN_CONV))
        for h in range(N_HEADS):
            q = qkv[:, h * hd:(h + 1) * hd]
            k = qkv[:, width + h * hd:width + (h + 1) * hd]
            v = qkv[:, 2 * width + h * hd:2 * width + (h + 1) * hd]
            q = q * lax.rsqrt(jnp.sum(q * q, axis=-1, keepdims=True) + EPS) * (hd ** -0.5)
            k = k * lax.rsqrt(jnp.sum(k * k, axis=-1, keepdims=True) + EPS)
            beta = beta_all[r0:r0 + c, h:h + 1]
            gc_col = gc_all[r0:r0 + c, N_HEADS + h:N_HEADS + h + 1]
            gc_row = gc_t[N_HEADS + h:N_HEADS + h + 1, r0:r0 + c]
            exp_gc = jnp.exp(gc_col)
            gc_last = gc_col[c - 1:c, :]
            k_beta = k * beta
            qs.append(q)
            ks.append(k)
            kbs.append(k_beta)
            decays.append(jnp.where(causal, jnp.exp(jnp.where(causal, gc_col - gc_row, 0.0)), 0.0))
            rhss.append(jnp.concatenate([v * beta, k_beta * exp_gc], axis=1))
            qds.append(q * exp_gc)
            kds.append(k * jnp.exp(gc_last - gc_col))
            carries.append(jnp.exp(gc_last))
    q_all, k_all, decay = jnp.stack(qs), jnp.stack(ks), jnp.stack(decays)
    low = jnp.where(strict, _bmm_nt(jnp.stack(kbs), k_all) * decay, 0.0)
    uw = _bmm(_inv_unit_lower(low, ii, jj), jnp.stack(rhss))
    attn = _bmm_nt(q_all, k_all) * decay

    state = state_ref[...]
    for ci in range(tc // c):
        r0 = ci * c
        sl = slice(ci * N_HEADS, (ci + 1) * N_HEADS)
        wq = _bmm(jnp.concatenate([uw[sl, :, hd:], jnp.stack(qds[sl])], axis=1), state)
        v_new = uw[sl, :, :hd] - wq[:, :c]
        o = wq[:, c:] + _bmm(attn[sl], v_new)
        state = jnp.stack([state[h] * carries[ci * N_HEADS + h]
                           + _mm_tn(kds[ci * N_HEADS + h], v_new[h]) for h in range(N_HEADS)])
        for h in range(N_HEADS):
            zg = _silu(p_ref[r0:r0 + c, 3 * width + h * hd:3 * width + (h + 1) * hd])
            o_ref[r0:r0 + c, h * hd:(h + 1) * hd] = (_rms_rows(o[h], gain) * zg).astype(BF16)
    state_ref[...] = state
    halo_ref[...] = p_ref[tc - HALO:tc, :3 * width]


def _hgrn_block(p_ref, lbp, gain, o_ref, state_ref, tc, layer, out_col):
    c = HGRN_CHUNK
    hd = HEAD_DIM
    sup = HGRN_SUPER
    width = N_HEADS * hd
    n_sub = sup // c

    lbe = jnp.exp(lbp - jnp.max(lbp, axis=0, keepdims=True))
    lbs = lbe / jnp.sum(lbe, axis=0, keepdims=True)
    lb_all = jnp.sum(lbs[0:layer + 1], axis=0, keepdims=True) - lbs[0:1]

    half = c // 2
    pos = lax.broadcasted_iota(jnp.int32, (sup, hd), 0) % c
    tile_row = lax.broadcasted_iota(jnp.int32, (half, hd), 0)
    n_pair = half * half + half * c
    gi = lax.broadcasted_iota(jnp.int32, (c, n_pair), 0)
    gj = lax.broadcasted_iota(jnp.int32, (c, n_pair), 1)
    owner = jnp.where(gj < half * half, gj // half, half + (gj - half * half) // c)
    group = jnp.broadcast_to((owner == gi).astype(BF16), (n_sub, c, n_pair))
    ones = jnp.ones((hd, hd), BF16)

    for h in range(N_HEADS):
        lb = lb_all[:, h * hd:(h + 1) * hd]
        st = state_ref[h]
        for r0 in range(0, tc, sup):
            col = lambda part: slice(part * width + h * hd, part * width + (h + 1) * hd)
            f = lb + (1.0 - lb) * _sigmoid(p_ref[r0:r0 + sup, col(1)])
            log_f = jnp.log(jnp.maximum(f, F_FLOOR))
            kk = 1.0 - f
            q = _silu(p_ref[r0:r0 + sup, col(0)])
            v = p_ref[r0:r0 + sup, col(2)]
            b = log_f
            tail = log_f
            s = 1
            while s < c:
                b = b + jnp.where(pos >= s, pltpu.roll(b, s, 0), 0.0)
                tail = tail + jnp.where(pos < c - s, pltpu.roll(tail, sup - s, 0), 0.0)
                s *= 2
            e_b = jnp.exp(b)
            q_dec = q * e_b
            k_dec = kk * jnp.exp(tail - log_f)
            pairs, v_reps = [], []
            for sc in range(n_sub):
                lo = sc * c
                b_a, b_b = b[lo:lo + half], b[lo + half:lo + c]
                k_a, k_b = kk[lo:lo + half], kk[lo + half:lo + c]
                v_a = v[lo:lo + half]
                v_c = v[lo:lo + c]
                for i in range(c):
                    b_i = b[lo + i:lo + i + 1]
                    q_i = q[lo + i:lo + i + 1]
                    if i < half:
                        pairs.append(jnp.exp(jnp.where(tile_row <= i, b_i - b_a, -1e30)) * k_a * q_i)
                        v_reps.append(v_a)
                    else:
                        pairs.append(jnp.exp(b_i - b_a) * k_a * q_i)
                        pairs.append(jnp.exp(jnp.where(tile_row <= i - half, b_i - b_b, -1e30))
                                     * k_b * q_i)
                        v_reps.append(v_c)
            scores = _mm(jnp.concatenate(pairs, axis=0), ones)
            weighted = (scores * jnp.concatenate(v_reps, axis=0)).reshape(n_sub, n_pair, hd)
            o_intra = _bmm(group, weighted)
            updates = [_mm_tn(v[sc * c:(sc + 1) * c], k_dec[sc * c:(sc + 1) * c])
                       for sc in range(n_sub)]
            states = [st]
            for sc in range(n_sub):
                states.append(states[sc] * e_b[sc * c + c - 1:sc * c + c] + updates[sc])
            o = jnp.concatenate([o_intra[sc] + _mm_nt(q_dec[sc * c:(sc + 1) * c], states[sc])
                                 for sc in range(n_sub)], axis=0)
            st = states[n_sub]
            gate = _silu(p_ref[r0:r0 + sup, col(3)])
            o_ref[r0:r0 + sup, out_col + h * hd:out_col + (h + 1) * hd] = (
                _rms_rows(o, gain) * gate).astype(BF16)
        state_ref[h] = st


def _even_kernel(x_ref, g_ref, wa_ref, wb_ref, ws_ref, wst_ref, cw_ref, alog_ref, dt_ref,
                 alog_t_ref, dt_t_ref, ggain_ref, lbp_ref, hgain_ref, o_ref,
                 pa_ref, pb_ref, gstate_ref, ghalo_ref, hstate_ref, *, tc, layer):
    @pl.when(pl.program_id(1) == 0)
    def _():
        gstate_ref[...] = jnp.zeros_like(gstate_ref)
        ghalo_ref[...] = jnp.zeros_like(ghalo_ref)
        hstate_ref[...] = jnp.zeros_like(hstate_ref)

    hn = _rms_rows(x_ref[...], g_ref[...]).astype(BF16)
    pa_ref[...] = jnp.dot(hn, wa_ref[...], preferred_element_type=F32)
    pb_ref[...] = jnp.dot(hn, wb_ref[...], preferred_element_type=F32)
    small = jnp.dot(hn, ws_ref[...], preferred_element_type=F32)
    small_t = _mm_nt(wst_ref[...], hn)
    _gdn_block(pa_ref, small, small_t, cw_ref[...], alog_ref[...], dt_ref[...], alog_t_ref[...],
               dt_t_ref[...], ggain_ref[...], o_ref, gstate_ref, ghalo_ref, tc)
    _hgrn_block(pb_ref, lbp_ref[...], hgain_ref[...], o_ref, hstate_ref, tc, layer,
                N_HEADS * HEAD_DIM)


def _even_mixer(x, gain, w_in, conv_w, a_log, dt_bias, gdn_gain, lower_bounds_raw, hgrn_gain,
                *, layer, batch, seq, tc=256):
    m, d = x.shape
    nt = seq // tc
    width = N_HEADS * HEAD_DIM
    split = 4 * width
    n_t = 2 * HALO
    w_a = w_in[:, :split].astype(BF16)
    w_b = w_in[:, split + 2 * N_HEADS:].astype(BF16)
    w_gates = w_in[:, split:split + 2 * N_HEADS]
    w_small = jnp.pad(w_gates, ((0, 0), (0, HEAD_DIM - 2 * N_HEADS))).astype(BF16)
    w_small_t = jnp.pad(w_gates.T, ((0, n_t - 2 * N_HEADS), (0, 0))).astype(BF16)
    lane_vec = lambda v: jnp.pad(v.astype(F32), (N_HEADS, HEAD_DIM - 2 * N_HEADS)).reshape(1, HEAD_DIM)
    row_vec = lambda v: jnp.pad(v.astype(F32), (N_HEADS, n_t - 2 * N_HEADS)).reshape(n_t, 1)
    row = lambda b, t: (b * nt + t, 0)
    const = lambda b, t: (0, 0)
    return pl.pallas_call(
        functools.partial(_even_kernel, tc=tc, layer=layer),
        grid=(batch, nt),
        in_specs=[pl.BlockSpec((tc, d), row),
                  pl.BlockSpec((1, d), const),
                  pl.BlockSpec((d, split), const, **RESIDENT),
                  pl.BlockSpec((d, split), const, **RESIDENT),
                  pl.BlockSpec((d, HEAD_DIM), const),
                  pl.BlockSpec((n_t, d), const),
                  pl.BlockSpec((GDN_CONV, 3 * width), const),
                  pl.BlockSpec((1, HEAD_DIM), const),
                  pl.BlockSpec((1, HEAD_DIM), const),
                  pl.BlockSpec((n_t, 1), const),
                  pl.BlockSpec((n_t, 1), const),
                  pl.BlockSpec((1, HEAD_DIM), const),
                  pl.BlockSpec(lower_bounds_raw.shape, const),
                  pl.BlockSpec((1, HEAD_DIM), const)],
        out_specs=pl.BlockSpec((tc, 2 * width), row),
        out_shape=jax.ShapeDtypeStruct((m, 2 * width), BF16),
        scratch_shapes=[pltpu.VMEM((tc, split), F32),
                        pltpu.VMEM((tc, split), F32),
                        pltpu.VMEM((N_HEADS, HEAD_DIM, HEAD_DIM), F32),
                        pltpu.VMEM((HALO, 3 * width), F32),
                        pltpu.VMEM((N_HEADS, HEAD_DIM, HEAD_DIM), F32)],
        compiler_params=_params("parallel", "arbitrary"),
        name="even_mixer",
    )(x, gain.reshape(1, d), w_a, w_b, w_small, w_small_t, conv_w, lane_vec(a_log),
      lane_vec(dt_bias), row_vec(a_log), row_vec(dt_bias), gdn_gain.reshape(1, HEAD_DIM),
      lower_bounds_raw, hgrn_gain.reshape(1, HEAD_DIM))


def _rglru_kernel(res_ref, g_ref, win_ref, cw_ref, cb_ref, wa_ref, ba_ref, wx_ref, bx_ref,
                  lam_ref, o_ref, h_ref, halo_ref, *, tc):
    n_heads = wa_ref.shape[0]
    blk = LRU_BLOCK
    width = n_heads * blk

    @pl.when(pl.program_id(1) == 0)
    def _():
        h_ref[...] = jnp.zeros_like(h_ref)
        halo_ref[...] = jnp.zeros_like(halo_ref)

    hn = _rms_rows(res_ref[...], g_ref[...]).astype(BF16)
    y = jnp.dot(hn, win_ref[:, :width], preferred_element_type=F32)
    x = jnp.dot(hn, win_ref[:, width:], preferred_element_type=F32)
    ext = jnp.concatenate([halo_ref[...], x], axis=0)
    halo_ref[...] = x[tc - HALO:tc]
    xc = _causal_conv(ext, cw_ref[...], LRU_CONV) + cb_ref[...]
    ra = jnp.concatenate([_mm(xc[:, h * blk:(h + 1) * blk], wa_ref[h]) for h in range(n_heads)],
                         axis=1)
    rx = jnp.concatenate([_mm(xc[:, h * blk:(h + 1) * blk], wx_ref[h]) for h in range(n_heads)],
                         axis=1)
    r = _sigmoid(ra + ba_ref[...])
    gi = _sigmoid(rx + bx_ref[...])
    log_a = -RG_C * r * _softplus(-lam_ref[...])
    a = jnp.exp(log_a)
    th = jnp.tanh(log_a)
    one_minus_a2 = -2.0 * th / (1.0 - th)
    u = jnp.sqrt(jnp.maximum(one_minus_a2, 0.0)) * (gi * xc)

    sub = lax.broadcasted_iota(jnp.int32, a.shape, 0) % HALO
    for s in (1, 2, 4):
        a_prev = jnp.where(sub >= s, pltpu.roll(a, s, 0), 1.0)
        u_prev = jnp.where(sub >= s, pltpu.roll(u, s, 0), 0.0)
        u = a * u_prev + u
        a = a * a_prev
    carry = h_ref[0:1, :]
    tiles = []
    for t in range(tc // HALO):
        ht = a[t * HALO:(t + 1) * HALO] * carry + u[t * HALO:(t + 1) * HALO]
        tiles.append(ht)
        carry = ht[HALO - 1:HALO]
    h_ref[...] = jnp.broadcast_to(carry, h_ref.shape)
    hs = jnp.concatenate(tiles, axis=0)
    gate = 0.5 * y * (1.0 + jnp.tanh(0.7978845608028654 * (y + 0.044715 * (y * y * y))))
    o_ref[...] = (hs * gate).astype(BF16)


def _rglru(x, gain, w_in, conv_w, conv_b, wa, ba, wx, bx, lam, *, batch, seq, tc=256):
    m, d = x.shape
    nt = seq // tc
    width = conv_w.shape[1]
    const = lambda b, t: (0, 0)
    row = lambda b, t: (b * nt + t, 0)
    vec = pl.BlockSpec((1, width), const)
    mat = pl.BlockSpec(wa.shape, lambda b, t: (0, 0, 0))
    return pl.pallas_call(
        functools.partial(_rglru_kernel, tc=tc),
        grid=(batch, nt),
        in_specs=[pl.BlockSpec((tc, d), row),
                  pl.BlockSpec((1, d), const),
                  pl.BlockSpec(w_in.shape, const, **RESIDENT),
                  pl.BlockSpec((LRU_CONV, width), const),
                  vec, mat, vec, mat, vec, vec],
        out_specs=pl.BlockSpec((tc, width), row),
        out_shape=jax.ShapeDtypeStruct((m, width), BF16),
        scratch_shapes=[pltpu.VMEM((HALO, width), F32), pltpu.VMEM((HALO, width), F32)],
        compiler_params=_params("parallel", "arbitrary"),
        name="rglru",
    )(x, gain.reshape(1, d), w_in, conv_w, conv_b.reshape(1, width), wa, ba.reshape(1, width),
      wx, bx.reshape(1, width), lam.reshape(1, width))


def _ffn_kernel(x_ref, mix_ref, wmix_ref, g_ref, wu_ref, cw_ref, cb_ref, wd_ref, gf_ref, o_ref,
                xm_ref, h_ref, act_ref, halo_ref, *, tiles_per_seq, tf, final_norm):
    d_ff = wd_ref.shape[0]

    @pl.when(pl.program_id(0) % tiles_per_seq == 0)
    def _():
        halo_ref[...] = jnp.zeros_like(halo_ref)

    xm = x_ref[...] + jnp.dot(mix_ref[...], wmix_ref[...], preferred_element_type=F32)
    xm_ref[...] = xm
    h_ref[...] = _rms_rows(xm, g_ref[...]).astype(BF16)
    for c0 in range(0, d_ff, tf):
        h = h_ref[...]
        gate = jnp.dot(h, wu_ref[:, c0:c0 + tf], preferred_element_type=F32)
        val = jnp.dot(h, wu_ref[:, d_ff + c0:d_ff + c0 + tf], preferred_element_type=F32)
        ext = jnp.concatenate([halo_ref[:, c0:c0 + tf], gate], axis=0)
        halo_ref[:, c0:c0 + tf] = gate[gate.shape[0] - HALO:]
        gc = _causal_conv(ext, cw_ref[:, c0:c0 + tf], FFN_CONV) + cb_ref[:, c0:c0 + tf]
        act_ref[:, c0:c0 + tf] = (_silu(gc) * val).astype(BF16)
    acc = xm_ref[...] + jnp.dot(act_ref[...], wd_ref[...], preferred_element_type=F32)
    if final_norm:
        acc = _rms_rows(acc, gf_ref[...])
    o_ref[...] = acc


def _mix_ffn(x, mix, w_mix, gain, w_up, conv_w, conv_b, w_down, final_gain, *, seq, final_norm,
             tm=1024, tf=256):
    m, d = x.shape
    d_ff = w_down.shape[0]
    row = lambda i: (i, 0)
    const = lambda i: (0, 0)
    return pl.pallas_call(
        functools.partial(_ffn_kernel, tiles_per_seq=seq // tm, tf=tf, final_norm=final_norm),
        grid=(m // tm,),
        in_specs=[pl.BlockSpec((tm, d), row),
                  pl.BlockSpec((tm, mix.shape[1]), row),
                  pl.BlockSpec(w_mix.shape, const, **RESIDENT),
                  pl.BlockSpec((1, d), const),
                  pl.BlockSpec((d, 2 * d_ff), const, **RESIDENT),
                  pl.BlockSpec((FFN_CONV, d_ff), const),
                  pl.BlockSpec((1, d_ff), const),
                  pl.BlockSpec((d_ff, d), const, **RESIDENT),
                  pl.BlockSpec((1, d), const)],
        out_specs=pl.BlockSpec((tm, d), row),
        out_shape=jax.ShapeDtypeStruct((m, d), F32),
        scratch_shapes=[pltpu.VMEM((tm, d), F32),
                        pltpu.VMEM((tm, d), BF16),
                        pltpu.VMEM((tm, d_ff), BF16),
                        pltpu.VMEM((HALO, d_ff), F32)],
        compiler_params=_params("arbitrary"),
        name="mix_ffn",
    )(x, mix, w_mix, gain.reshape(1, d), w_up, conv_w, conv_b.reshape(1, d_ff), w_down,
      final_gain.reshape(1, d))


def kernel(x, norm_mix, norm_ffn, norm_final, ab_w_in, gdn_conv_w, gdn_a_log, gdn_dt_bias,
           gdn_norm, hgrn_lower_bounds, hgrn_norm, ab_w_out, c_w_in, c_conv_w, c_conv_b,
           c_gate_a_w, c_gate_a_b, c_gate_x_w, c_gate_x_b, c_lambda, c_w_out,
           ffn_w_up, ffn_conv_w, ffn_conv_b, ffn_w_down):
    batch, seq, d = x.shape
    depth = norm_mix.shape[0]
    x = x.reshape(batch * seq, d)
    for layer in range(depth):
        j = layer // 2
        if layer % 2 == 0:
            mix = _even_mixer(x, norm_mix[layer], ab_w_in[j], gdn_conv_w[j], gdn_a_log[j],
                              gdn_dt_bias[j], gdn_norm[j], hgrn_lower_bounds, hgrn_norm[j],
                              layer=j, batch=batch, seq=seq)
            w_mix = ab_w_out[j]
        else:
            mix = _rglru(x, norm_mix[layer], c_w_in[j].astype(BF16), c_conv_w[j], c_conv_b[j],
                         c_gate_a_w[j].astype(BF16), c_gate_a_b[j], c_gate_x_w[j].astype(BF16),
                         c_gate_x_b[j], c_lambda[j], batch=batch, seq=seq)
            w_mix = c_w_out[j]
        x = _mix_ffn(x, mix, w_mix.astype(BF16), norm_ffn[layer], ffn_w_up[layer].astype(BF16),
                     ffn_conv_w[layer], ffn_conv_b[layer], ffn_w_down[layer].astype(BF16),
                     norm_final, seq=seq, final_norm=layer == depth - 1)
    return x.reshape(batch, seq, d)
```

```python
import functools

import jax
import jax.numpy as jnp
from jax import lax
from jax.experimental import pallas as pl
from jax.experimental.pallas import tpu as pltpu

F32 = jnp.float32
BF16 = jnp.bfloat16

EPS = 1e-6
F_FLOOR = 1e-30
RG_C = 8.0

HEAD_DIM = 128
N_HEADS = 4
GDN_CHUNK = 64
GDN_CONV = 4
HGRN_CHUNK = 16
HGRN_SUPER = 128
LRU_BLOCK = 256
LRU_CONV = 4
FFN_CONV = 3
HALO = 8

VMEM_LIMIT_BYTES = 56 * 1024 * 1024
RESIDENT = dict(pipeline_mode=pl.Buffered(1))


def _params(*semantics):
    return pltpu.CompilerParams(dimension_semantics=semantics,
                                vmem_limit_bytes=VMEM_LIMIT_BYTES)


def _sigmoid(x):
    return 1.0 / (1.0 + jnp.exp(-x))


def _silu(x):
    return x * _sigmoid(x)


def _softplus(x):
    return jnp.maximum(x, 0.0) + jnp.log(1.0 + jnp.exp(-jnp.abs(x)))


def _mm(a, b):
    return jnp.dot(a.astype(BF16), b.astype(BF16), preferred_element_type=F32)


def _mm_nt(a, b):
    return lax.dot_general(a.astype(BF16), b.astype(BF16), (((1,), (1,)), ((), ())),
                           preferred_element_type=F32)


def _mm_tn(a, b):
    return lax.dot_general(a.astype(BF16), b.astype(BF16), (((0,), (0,)), ((), ())),
                           preferred_element_type=F32)


def _bmm(a, b):
    return lax.dot_general(a.astype(BF16), b.astype(BF16), (((2,), (1,)), ((0,), (0,))),
                           preferred_element_type=F32)


def _bmm_nt(a, b):
    return lax.dot_general(a.astype(BF16), b.astype(BF16), (((2,), (2,)), ((0,), (0,))),
                           preferred_element_type=F32)


def _rms_rows(x, gain):
    ms = jnp.mean(x * x, axis=-1, keepdims=True)
    return x * lax.rsqrt(ms + EPS) * gain


def _causal_conv(ext, w, width):
    out = ext[HALO:] * w[width - 1:width]
    for k in range(width - 1):
        shift = width - 1 - k
        out = out + pltpu.roll(ext, shift, 0)[HALO:] * w[k:k + 1]
    return out


def _inv_unit_lower(low, ii, jj):
    eye = (ii == jj).astype(F32)
    same_block = (ii // 16) == (jj // 16)
    d1 = jnp.where(same_block, low, 0.0)
    e = jnp.where(same_block, 0.0, low)
    d2 = _bmm(d1, d1)
    d4 = _bmm(d2, d2)
    d8 = _bmm(d4, d4)
    dinv = _bmm(eye - d1, eye + d2)
    dinv = _bmm(dinv, eye + d4)
    dinv = _bmm(dinv, eye + d8)
    n1 = _bmm(dinv, e)
    n2 = _bmm(n1, n1)
    return _bmm(_bmm(eye - n1, eye + n2), dinv)


def _chunk_cumsum(x, axis):
    pos = lax.broadcasted_iota(jnp.int32, x.shape, axis) % GDN_CHUNK
    s = 1
    while s < GDN_CHUNK:
        x = x + jnp.where(pos >= s, pltpu.roll(x, s, axis), 0.0)
        s *= 2
    return x


def _gdn_block(p_ref, small, small_t, cw, alog, dt, alog_t, dt_t, gain, o_ref, state_ref,
               halo_ref, tc):
    c = GDN_CHUNK
    hd = HEAD_DIM
    width = N_HEADS * hd
    ii = lax.broadcasted_iota(jnp.int32, (c, c), 0)
    jj = lax.broadcasted_iota(jnp.int32, (c, c), 1)
    causal = ii >= jj
    strict = ii > jj

    beta_all = _sigmoid(small)
    gc_all = _chunk_cumsum(-jnp.exp(alog) * _softplus(small + dt), 0)
    gc_t = _chunk_cumsum(-jnp.exp(alog_t) * _softplus(small_t + dt_t), 1)

    qs, ks, kbs, decays, rhss, qds, kds, carries = [], [], [], [], [], [], [], []
    for ci in range(tc // c):
        r0 = ci * c
        prev = halo_ref[...] if ci == 0 else p_ref[r0 - HALO:r0, :3 * width]
        ext = jnp.concatenate([prev, p_ref[r0:r0 + c, :3 * width]], axis=0)
        qkv = _silu(_causal_conv(ext, cw, GDN_CONV))
        for h in range(N_HEADS):
            q = qkv[:, h * hd:(h + 1) * hd]
            k = qkv[:, width + h * hd:width + (h + 1) * hd]
            v = qkv[:, 2 * width + h * hd:2 * width + (h + 1) * hd]
            q = q * lax.rsqrt(jnp.sum(q * q, axis=-1, keepdims=True) + EPS) * (hd ** -0.5)
            k = k * lax.rsqrt(jnp.sum(k * k, axis=-1, keepdims=True) + EPS)
            beta = beta_all[r0:r0 + c, h:h + 1]
            gc_col = gc_all[r0:r0 + c, N_HEADS + h:N_HEADS + h + 1]
            gc_row = gc_t[N_HEADS + h:N_HEADS + h + 1, r0:r0 + c]
            exp_gc = jnp.exp(gc_col)
            gc_last = gc_col[c - 1:c, :]
            k_beta = k * beta
            qs.append(q)
            ks.append(k)
            kbs.append(k_beta)
            decays.append(jnp.where(causal, jnp.exp(jnp.where(causal, gc_col - gc_row, 0.0)), 0.0))
            rhss.append(jnp.concatenate([v * beta, k_beta * exp_gc], axis=1))
            qds.append(q * exp_gc)
            kds.append(k * jnp.exp(gc_last - gc_col))
            carries.append(jnp.exp(gc_last))
    q_all, k_all, decay = jnp.stack(qs), jnp.stack(ks), jnp.stack(decays)
    low = jnp.where(strict, _bmm_nt(jnp.stack(kbs), k_all) * decay, 0.0)
    uw = _bmm(_inv_unit_lower(low, ii, jj), jnp.stack(rhss))
    attn = _bmm_nt(q_all, k_all) * decay

    state = state_ref[...]
    for ci in range(tc // c):
        r0 = ci * c
        sl = slice(ci * N_HEADS, (ci + 1) * N_HEADS)
        wq = _bmm(jnp.concatenate([uw[sl, :, hd:], jnp.stack(qds[sl])], axis=1), state)
        v_new = uw[sl, :, :hd] - wq[:, :c]
        o = wq[:, c:] + _bmm(attn[sl], v_new)
        state = jnp.stack([state[h] * carries[ci * N_HEADS + h]
                           + _mm_tn(kds[ci * N_HEADS + h], v_new[h]) for h in range(N_HEADS)])
        for h in range(N_HEADS):
            zg = _silu(p_ref[r0:r0 + c, 3 * width + h * hd:3 * width + (h + 1) * hd])
            o_ref[r0:r0 + c, h * hd:(h + 1) * hd] = (_rms_rows(o[h], gain) * zg).astype(BF16)
    state_ref[...] = state
    halo_ref[...] = p_ref[tc - HALO:tc, :3 * width]


def _hgrn_block(p_ref, lbp, gain, o_ref, state_ref, tc, layer, out_col):
    c = HGRN_CHUNK
    hd = HEAD_DIM
    sup = HGRN_SUPER
    width = N_HEADS * hd
    n_sub = sup // c

    lbe = jnp.exp(lbp - jnp.max(lbp, axis=0, keepdims=True))
    lbs = lbe / jnp.sum(lbe, axis=0, keepdims=True)
    lb_all = jnp.sum(lbs[0:layer + 1], axis=0, keepdims=True) - lbs[0:1]

    half = c // 2
    pos = lax.broadcasted_iota(jnp.int32, (sup, hd), 0) % c
    tile_row = lax.broadcasted_iota(jnp.int32, (half, hd), 0)
    n_pair = half * half + half * c
    gi = lax.broadcasted_iota(jnp.int32, (c, n_pair), 0)
    gj = lax.broadcasted_iota(jnp.int32, (c, n_pair), 1)
    owner = jnp.where(gj < half * half, gj // half, half + (gj - half * half) // c)
    group = jnp.broadcast_to((owner == gi).astype(BF16), (n_sub, c, n_pair))
    ones = jnp.ones((hd, hd), BF16)

    for h in range(N_HEADS):
        lb = lb_all[:, h * hd:(h + 1) * hd]
        st = state_ref[h]
        for r0 in range(0, tc, sup):
            col = lambda part: slice(part * width + h * hd, part * width + (h + 1) * hd)
            f = lb + (1.0 - lb) * _sigmoid(p_ref[r0:r0 + sup, col(1)])
            log_f = jnp.log(jnp.maximum(f, F_FLOOR))
            kk = 1.0 - f
            q = _silu(p_ref[r0:r0 + sup, col(0)])
            v = p_ref[r0:r0 + sup, col(2)]
            b = log_f
            tail = log_f
            s = 1
            while s < c:
                b = b + jnp.where(pos >= s, pltpu.roll(b, s, 0), 0.0)
                tail = tail + jnp.where(pos < c - s, pltpu.roll(tail, sup - s, 0), 0.0)
                s *= 2
            e_b = jnp.exp(b)
            q_dec = q * e_b
            k_dec = kk * jnp.exp(tail - log_f)
            pairs, v_reps = [], []
            for sc in range(n_sub):
                lo = sc * c
                b_a, b_b = b[lo:lo + half], b[lo + half:lo + c]
                k_a, k_b = kk[lo:lo + half], kk[lo + half:lo + c]
                v_a = v[lo:lo + half]
                v_c = v[lo:lo + c]
                for i in range(c):
                    b_i = b[lo + i:lo + i + 1]
                    q_i = q[lo + i:lo + i + 1]
                    if i < half:
                        pairs.append(jnp.exp(jnp.where(tile_row <= i, b_i - b_a, -1e30)) * k_a * q_i)
                        v_reps.append(v_a)
                    else:
                        pairs.append(jnp.exp(b_i - b_a) * k_a * q_i)
                        pairs.append(jnp.exp(jnp.where(tile_row <= i - half, b_i - b_b, -1e30))
                                     * k_b * q_i)
                        v_reps.append(v_c)
            scores = _mm(jnp.concatenate(pairs, axis=0), ones)
            weighted = (scores * jnp.concatenate(v_reps, axis=0)).reshape(n_sub, n_pair, hd)
            o_intra = _bmm(group, weighted)
            updates = [_mm_tn(v[sc * c:(sc + 1) * c], k_dec[sc * c:(sc + 1) * c])
                       for sc in range(n_sub)]
            states = [st]
            for sc in range(n_sub):
                states.append(states[sc] * e_b[sc * c + c - 1:sc * c + c] + updates[sc])
            o = jnp.concatenate([o_intra[sc] + _mm_nt(q_dec[sc * c:(sc + 1) * c], states[sc])
                                 for sc in range(n_sub)], axis=0)
            st = states[n_sub]
            gate = _silu(p_ref[r0:r0 + sup, col(3)])
            o_ref[r0:r0 + sup, out_col + h * hd:out_col + (h + 1) * hd] = (
                _rms_rows(o, gain) * gate).astype(BF16)
        state_ref[h] = st


def _even_kernel(x_ref, g_ref, wa_ref, wb_ref, ws_ref, wst_ref, cw_ref, alog_ref, dt_ref,
                 alog_t_ref, dt_t_ref, ggain_ref, lbp_ref, hgain_ref, o_ref,
                 pa_ref, pb_ref, gstate_ref, ghalo_ref, hstate_ref, *, tc, layer):
    @pl.when(pl.program_id(1) == 0)
    def _():
        gstate_ref[...] = jnp.zeros_like(gstate_ref)
        ghalo_ref[...] = jnp.zeros_like(ghalo_ref)
        hstate_ref[...] = jnp.zeros_like(hstate_ref)

    hn = _rms_rows(x_ref[...], g_ref[...]).astype(BF16)
    pa_ref[...] = jnp.dot(hn, wa_ref[...], preferred_element_type=F32)
    pb_ref[...] = jnp.dot(hn, wb_ref[...], preferred_element_type=F32)
    small = jnp.dot(hn, ws_ref[...], preferred_element_type=F32)
    small_t = _mm_nt(wst_ref[...], hn)
    _gdn_block(pa_ref, small, small_t, cw_ref[...], alog_ref[...], dt_ref[...], alog_t_ref[...],
               dt_t_ref[...], ggain_ref[...], o_ref, gstate_ref, ghalo_ref, tc)
    _hgrn_block(pb_ref, lbp_ref[...], hgain_ref[...], o_ref, hstate_ref, tc, layer,
                N_HEADS * HEAD_DIM)


def _split_even_in_proj(w_in):
    split = 4 * N_HEADS * HEAD_DIM
    return (w_in[:, :, :split].astype(BF16), w_in[:, :, split + 2 * N_HEADS:].astype(BF16),
            w_in[:, :, split:split + 2 * N_HEADS])


def _even_mixer(x, gain, w_a, w_b, w_gates, conv_w, a_log, dt_bias, gdn_gain, lower_bounds_raw,
                hgrn_gain, *, layer, batch, seq, tc=512):
    m, d = x.shape
    nt = seq // tc
    width = N_HEADS * HEAD_DIM
    split = 4 * width
    n_t = 2 * HALO
    w_small = jnp.pad(w_gates, ((0, 0), (0, HEAD_DIM - 2 * N_HEADS))).astype(BF16)
    w_small_t = jnp.pad(w_gates.T, ((0, n_t - 2 * N_HEADS), (0, 0))).astype(BF16)
    lane_vec = lambda v: jnp.pad(v.astype(F32), (N_HEADS, HEAD_DIM - 2 * N_HEADS)).reshape(1, HEAD_DIM)
    row_vec = lambda v: jnp.pad(v.astype(F32), (N_HEADS, n_t - 2 * N_HEADS)).reshape(n_t, 1)
    row = lambda b, t: (b * nt + t, 0)
    const = lambda b, t: (0, 0)
    return pl.pallas_call(
        functools.partial(_even_kernel, tc=tc, layer=layer),
        grid=(batch, nt),
        in_specs=[pl.BlockSpec((tc, d), row),
                  pl.BlockSpec((1, d), const),
                  pl.BlockSpec((None, d, split), lambda b, t: (layer, 0, 0), **RESIDENT),
                  pl.BlockSpec((None, d, split), lambda b, t: (layer, 0, 0), **RESIDENT),
                  pl.BlockSpec((d, HEAD_DIM), const),
                  pl.BlockSpec((n_t, d), const),
                  pl.BlockSpec((GDN_CONV, 3 * width), const),
                  pl.BlockSpec((1, HEAD_DIM), const),
                  pl.BlockSpec((1, HEAD_DIM), const),
                  pl.BlockSpec((n_t, 1), const),
                  pl.BlockSpec((n_t, 1), const),
                  pl.BlockSpec((1, HEAD_DIM), const),
                  pl.BlockSpec(lower_bounds_raw.shape, const),
                  pl.BlockSpec((1, HEAD_DIM), const)],
        out_specs=pl.BlockSpec((tc, 2 * width), row),
        out_shape=jax.ShapeDtypeStruct((m, 2 * width), BF16),
        scratch_shapes=[pltpu.VMEM((tc, split), F32),
                        pltpu.VMEM((tc, split), F32),
                        pltpu.VMEM((N_HEADS, HEAD_DIM, HEAD_DIM), F32),
                        pltpu.VMEM((HALO, 3 * width), F32),
                        pltpu.VMEM((N_HEADS, HEAD_DIM, HEAD_DIM), F32)],
        compiler_params=_params("parallel", "arbitrary"),
        name="even_mixer",
    )(x, gain.reshape(1, d), w_a, w_b, w_small, w_small_t, conv_w, lane_vec(a_log),
      lane_vec(dt_bias), row_vec(a_log), row_vec(dt_bias), gdn_gain.reshape(1, HEAD_DIM),
      lower_bounds_raw, hgrn_gain.reshape(1, HEAD_DIM))


def _rglru_kernel(res_ref, g_ref, win_ref, cw_ref, cb_ref, wa_ref, ba_ref, wx_ref, bx_ref,
                  lam_ref, o_ref, h_ref, halo_ref, *, tc):
    n_heads = wa_ref.shape[0]
    blk = LRU_BLOCK
    width = n_heads * blk

    @pl.when(pl.program_id(1) == 0)
    def _():
        h_ref[...] = jnp.zeros_like(h_ref)
        halo_ref[...] = jnp.zeros_like(halo_ref)

    hn = _rms_rows(res_ref[...], g_ref[...]).astype(BF16)
    y = jnp.dot(hn, win_ref[:, :width], preferred_element_type=F32)
    x = jnp.dot(hn, win_ref[:, width:], preferred_element_type=F32)
    ext = jnp.concatenate([halo_ref[...], x], axis=0)
    halo_ref[...] = x[tc - HALO:tc]
    xc = _causal_conv(ext, cw_ref[...], LRU_CONV) + cb_ref[...]
    ra = jnp.concatenate([_mm(xc[:, h * blk:(h + 1) * blk], wa_ref[h]) for h in range(n_heads)],
                         axis=1)
    rx = jnp.concatenate([_mm(xc[:, h * blk:(h + 1) * blk], wx_ref[h]) for h in range(n_heads)],
                         axis=1)
    r = _sigmoid(ra + ba_ref[...])
    gi = _sigmoid(rx + bx_ref[...])
    log_a = r * (-RG_C * _softplus(-lam_ref[...]))
    a = jnp.exp(log_a)
    th = jnp.tanh(log_a)
    one_minus_a2 = -2.0 * th / (1.0 - th)
    u = jnp.sqrt(jnp.maximum(one_minus_a2, 0.0)) * (gi * xc)

    first = lax.broadcasted_iota(jnp.int32, (HALO, width), 0) == 0
    carry = h_ref[0:1, :]
    tiles = []
    for t in range(tc // HALO):
        a_t = a[t * HALO:(t + 1) * HALO]
        ht = u[t * HALO:(t + 1) * HALO] + jnp.where(first, a_t * carry, 0.0)
        mult = jnp.where(first, 0.0, a_t)
        for s in (1, 2, 4):
            ht = ht + mult * pltpu.roll(ht, s, 0)
            if s < HALO // 2:
                mult = mult * pltpu.roll(mult, s, 0)
        tiles.append(ht)
        carry = ht[HALO - 1:HALO]
    h_ref[...] = jnp.broadcast_to(carry, h_ref.shape)
    hs = jnp.concatenate(tiles, axis=0)
    gate = 0.5 * y * (1.0 + jnp.tanh(0.7978845608028654 * (y + 0.044715 * (y * y * y))))
    o_ref[...] = (hs * gate).astype(BF16)


def _rglru(x, gain, w_in, conv_w, conv_b, wa, ba, wx, bx, lam, *, layer, batch, seq, tc=256):
    m, d = x.shape
    nt = seq // tc
    width = conv_w.shape[1]
    const = lambda b, t: (0, 0)
    row = lambda b, t: (b * nt + t, 0)
    vec = pl.BlockSpec((1, width), const)
    mat = pl.BlockSpec((None,) + wa.shape[1:], lambda b, t: (layer, 0, 0, 0))
    return pl.pallas_call(
        functools.partial(_rglru_kernel, tc=tc),
        grid=(batch, nt),
        in_specs=[pl.BlockSpec((tc, d), row),
                  pl.BlockSpec((1, d), const),
                  pl.BlockSpec((None,) + w_in.shape[1:], lambda b, t: (layer, 0, 0), **RESIDENT),
                  pl.BlockSpec((LRU_CONV, width), const),
                  vec, mat, vec, mat, vec, vec],
        out_specs=pl.BlockSpec((tc, width), row),
        out_shape=jax.ShapeDtypeStruct((m, width), BF16),
        scratch_shapes=[pltpu.VMEM((HALO, width), F32), pltpu.VMEM((HALO, width), F32)],
        compiler_params=_params("parallel", "arbitrary"),
        name="rglru",
    )(x, gain.reshape(1, d), w_in, conv_w, conv_b.reshape(1, width), wa, ba.reshape(1, width),
      wx, bx.reshape(1, width), lam.reshape(1, width))


def _ffn_kernel(x_ref, mix_ref, wmix_ref, g_ref, wu_ref, cw_ref, cb_ref, wd_ref, gf_ref, o_ref,
                xm_ref, h_ref, act_ref, halo_ref, *, tiles_per_seq, tf, final_norm):
    d_ff = wd_ref.shape[0]

    @pl.when(pl.program_id(0) % tiles_per_seq == 0)
    def _():
        halo_ref[...] = jnp.zeros_like(halo_ref)

    xm = x_ref[...] + jnp.dot(mix_ref[...], wmix_ref[...], preferred_element_type=F32)
    xm_ref[...] = xm
    h_ref[...] = _rms_rows(xm, g_ref[...]).astype(BF16)
    for c0 in range(0, d_ff, tf):
        h = h_ref[...]
        gate = jnp.dot(h, wu_ref[:, c0:c0 + tf], preferred_element_type=F32)
        val = jnp.dot(h, wu_ref[:, d_ff + c0:d_ff + c0 + tf], preferred_element_type=F32)
        ext = jnp.concatenate([halo_ref[:, c0:c0 + tf], gate], axis=0)
        halo_ref[:, c0:c0 + tf] = gate[gate.shape[0] - HALO:]
        gc = _causal_conv(ext, cw_ref[:, c0:c0 + tf], FFN_CONV) + cb_ref[:, c0:c0 + tf]
        act_ref[:, c0:c0 + tf] = (_silu(gc) * val).astype(BF16)
    acc = xm_ref[...] + jnp.dot(act_ref[...], wd_ref[...], preferred_element_type=F32)
    if final_norm:
        acc = _rms_rows(acc, gf_ref[...])
    o_ref[...] = acc


def _mix_ffn(x, mix, w_mix, mix_layer, gain, w_up, conv_w, conv_b, w_down, final_gain, *, layer,
             seq, final_norm, tm=1024, tf=256):
    m, d = x.shape
    d_ff = w_down.shape[1]
    row = lambda i: (i, 0)
    const = lambda i: (0, 0)
    return pl.pallas_call(
        functools.partial(_ffn_kernel, tiles_per_seq=seq // tm, tf=tf, final_norm=final_norm),
        grid=(m // tm,),
        in_specs=[pl.BlockSpec((tm, d), row),
                  pl.BlockSpec((tm, mix.shape[1]), row),
                  pl.BlockSpec((None,) + w_mix.shape[1:], lambda i: (mix_layer, 0, 0), **RESIDENT),
                  pl.BlockSpec((1, d), const),
                  pl.BlockSpec((None, d, 2 * d_ff), lambda i: (layer, 0, 0), **RESIDENT),
                  pl.BlockSpec((FFN_CONV, d_ff), const),
                  pl.BlockSpec((1, d_ff), const),
                  pl.BlockSpec((None, d_ff, d), lambda i: (layer, 0, 0), **RESIDENT),
                  pl.BlockSpec((1, d), const)],
        out_specs=pl.BlockSpec((tm, d), row),
        out_shape=jax.ShapeDtypeStruct((m, d), F32),
        scratch_shapes=[pltpu.VMEM((tm, d), F32),
                        pltpu.VMEM((tm, d), BF16),
                        pltpu.VMEM((tm, d_ff), BF16),
                        pltpu.VMEM((HALO, d_ff), F32)],
        compiler_params=_params("arbitrary"),
        name="mix_ffn",
    )(x, mix, w_mix, gain.reshape(1, d), w_up, conv_w, conv_b.reshape(1, d_ff), w_down,
      final_gain.reshape(1, d))


def kernel(x, norm_mix, norm_ffn, norm_final, ab_w_in, gdn_conv_w, gdn_a_log, gdn_dt_bias,
           gdn_norm, hgrn_lower_bounds, hgrn_norm, ab_w_out, c_w_in, c_conv_w, c_conv_b,
           c_gate_a_w, c_gate_a_b, c_gate_x_w, c_gate_x_b, c_lambda, c_w_out,
           ffn_w_up, ffn_conv_w, ffn_conv_b, ffn_w_down):
    batch, seq, d = x.shape
    depth = norm_mix.shape[0]
    x = x.reshape(batch * seq, d)
    ab_w_a, ab_w_b, ab_w_gates = _split_even_in_proj(ab_w_in)
    ab_w_out, c_w_in, c_w_out = (w.astype(BF16) for w in (ab_w_out, c_w_in, c_w_out))
    c_gate_a_w, c_gate_x_w = c_gate_a_w.astype(BF16), c_gate_x_w.astype(BF16)
    ffn_w_up, ffn_w_down = ffn_w_up.astype(BF16), ffn_w_down.astype(BF16)
    for layer in range(depth):
        j = layer // 2
        if layer % 2 == 0:
            mix = _even_mixer(x, norm_mix[layer], ab_w_a, ab_w_b, ab_w_gates[j], gdn_conv_w[j],
                              gdn_a_log[j], gdn_dt_bias[j], gdn_norm[j], hgrn_lower_bounds,
                              hgrn_norm[j], layer=j, batch=batch, seq=seq)
            w_mix = ab_w_out
        else:
            mix = _rglru(x, norm_mix[layer], c_w_in, c_conv_w[j], c_conv_b[j], c_gate_a_w,
                         c_gate_a_b[j], c_gate_x_w, c_gate_x_b[j], c_lambda[j], layer=j,
                         batch=batch, seq=seq)
            w_mix = c_w_out
        x = _mix_ffn(x, mix, w_mix, j, norm_ffn[layer], ffn_w_up, ffn_conv_w[layer],
                     ffn_conv_b[layer], ffn_w_down, norm_final, layer=layer, seq=seq,
                     final_norm=layer == depth - 1)
    return x.reshape(batch, seq, d)
```

```python
import functools

import jax
import jax.numpy as jnp
from jax import lax
from jax.experimental import pallas as pl
from jax.experimental.pallas import tpu as pltpu

F32 = jnp.float32
BF16 = jnp.bfloat16

EPS = 1e-6
F_FLOOR = 1e-30
RG_C = 8.0

HEAD_DIM = 128
N_HEADS = 4
GDN_CHUNK = 64
GDN_CONV = 4
HGRN_CHUNK = 16
HGRN_SUPER = 256
LRU_BLOCK = 256
LRU_CONV = 4
FFN_CONV = 3
HALO = 8

VMEM_LIMIT_BYTES = 56 * 1024 * 1024
RESIDENT = dict(pipeline_mode=pl.Buffered(1))


def _params(*semantics):
    return pltpu.CompilerParams(dimension_semantics=semantics,
                                vmem_limit_bytes=VMEM_LIMIT_BYTES)


def _sigmoid(x):
    return 1.0 / (1.0 + jnp.exp(-x))


def _silu(x):
    return x * _sigmoid(x)


def _softplus(x):
    return jnp.maximum(x, 0.0) + jnp.log(1.0 + jnp.exp(-jnp.abs(x)))


def _mm(a, b):
    return jnp.dot(a.astype(BF16), b.astype(BF16), preferred_element_type=F32)


def _mm_nt(a, b):
    return lax.dot_general(a.astype(BF16), b.astype(BF16), (((1,), (1,)), ((), ())),
                           preferred_element_type=F32)


def _mm_tn(a, b):
    return lax.dot_general(a.astype(BF16), b.astype(BF16), (((0,), (0,)), ((), ())),
                           preferred_element_type=F32)


def _bmm(a, b):
    return lax.dot_general(a.astype(BF16), b.astype(BF16), (((2,), (1,)), ((0,), (0,))),
                           preferred_element_type=F32)


def _bmm_nt(a, b):
    return lax.dot_general(a.astype(BF16), b.astype(BF16), (((2,), (2,)), ((0,), (0,))),
                           preferred_element_type=F32)


def _rms_rows(x, gain):
    ms = jnp.mean(x * x, axis=-1, keepdims=True)
    return x * lax.rsqrt(ms + EPS) * gain


def _causal_conv(ext, w, width):
    out = ext[HALO:] * w[width - 1:width]
    for k in range(width - 1):
        shift = width - 1 - k
        out = out + pltpu.roll(ext, shift, 0)[HALO:] * w[k:k + 1]
    return out


def _inv_unit_lower(low, ii, jj):
    eye = (ii == jj).astype(F32)
    same_block = (ii // 16) == (jj // 16)
    d1 = jnp.where(same_block, low, 0.0)
    e = jnp.where(same_block, 0.0, low)
    d2 = _bmm(d1, d1)
    d4 = _bmm(d2, d2)
    d8 = _bmm(d4, d4)
    dinv = _bmm(eye - d1, eye + d2)
    dinv = _bmm(dinv, eye + d4)
    dinv = _bmm(dinv, eye + d8)
    n1 = _bmm(dinv, e)
    n2 = _bmm(n1, n1)
    return _bmm(_bmm(eye - n1, eye + n2), dinv)


def _chunk_cumsum(x, axis):
    pos = lax.broadcasted_iota(jnp.int32, x.shape, axis) % GDN_CHUNK
    s = 1
    while s < GDN_CHUNK:
        x = x + jnp.where(pos >= s, pltpu.roll(x, s, axis), 0.0)
        s *= 2
    return x


def _gdn_block(p_ref, small, small_t, cw, alog, dt, alog_t, dt_t, gain, o_ref, state_ref,
               halo_ref, tc):
    c = GDN_CHUNK
    hd = HEAD_DIM
    width = N_HEADS * hd
    ii = lax.broadcasted_iota(jnp.int32, (c, c), 0)
    jj = lax.broadcasted_iota(jnp.int32, (c, c), 1)
    causal = ii >= jj
    strict = ii > jj

    beta_all = _sigmoid(small)
    gc_all = _chunk_cumsum(-jnp.exp(alog) * _softplus(small + dt), 0)
    gc_t = _chunk_cumsum(-jnp.exp(alog_t) * _softplus(small_t + dt_t), 1)

    qs, ks, kbs, decays, rhss, qds, kds, carries = [], [], [], [], [], [], [], []
    for ci in range(tc // c):
        r0 = ci * c
        prev = halo_ref[...] if ci == 0 else p_ref[r0 - HALO:r0, :3 * width]
        ext = jnp.concatenate([prev, p_ref[r0:r0 + c, :3 * width]], axis=0)
        qkv = _silu(_causal_conv(ext, cw, GDN_CONV))
        for h in range(N_HEADS):
            q = qkv[:, h * hd:(h + 1) * hd]
            k = qkv[:, width + h * hd:width + (h + 1) * hd]
            v = qkv[:, 2 * width + h * hd:2 * width + (h + 1) * hd]
            q = q * lax.rsqrt(jnp.sum(q * q, axis=-1, keepdims=True) + EPS) * (hd ** -0.5)
            k = k * lax.rsqrt(jnp.sum(k * k, axis=-1, keepdims=True) + EPS)
            beta = beta_all[r0:r0 + c, h:h + 1]
            gc_col = gc_all[r0:r0 + c, N_HEADS + h:N_HEADS + h + 1]
            gc_row = gc_t[N_HEADS + h:N_HEADS + h + 1, r0:r0 + c]
            exp_gc = jnp.exp(gc_col)
            gc_last = gc_col[c - 1:c, :]
            k_beta = k * beta
            qs.append(q)
            ks.append(k)
            kbs.append(k_beta)
            decays.append(jnp.where(causal, jnp.exp(jnp.where(causal, gc_col - gc_row, 0.0)), 0.0))
            rhss.append(jnp.concatenate([v * beta, k_beta * exp_gc], axis=1))
            qds.append(q * exp_gc)
            kds.append(k * jnp.exp(gc_last - gc_col))
            carries.append(jnp.exp(gc_last))
    q_all, k_all, decay = jnp.stack(qs), jnp.stack(ks), jnp.stack(decays)
    low = jnp.where(strict, _bmm_nt(jnp.stack(kbs), k_all) * decay, 0.0)
    uw = _bmm(_inv_unit_lower(low, ii, jj), jnp.stack(rhss))
    attn = _bmm_nt(q_all, k_all) * decay

    state = state_ref[...]
    for ci in range(tc // c):
        r0 = ci * c
        sl = slice(ci * N_HEADS, (ci + 1) * N_HEADS)
        wq = _bmm(jnp.concatenate([uw[sl, :, hd:], jnp.stack(qds[sl])], axis=1), state)
        v_new = uw[sl, :, :hd] - wq[:, :c]
        o = wq[:, c:] + _bmm(attn[sl], v_new)
        state = jnp.stack([state[h] * carries[ci * N_HEADS + h]
                           + _mm_tn(kds[ci * N_HEADS + h], v_new[h]) for h in range(N_HEADS)])
        for h in range(N_HEADS):
            zg = _silu(p_ref[r0:r0 + c, 3 * width + h * hd:3 * width + (h + 1) * hd])
            o_ref[r0:r0 + c, h * hd:(h + 1) * hd] = (_rms_rows(o[h], gain) * zg).astype(BF16)
    state_ref[...] = state
    halo_ref[...] = p_ref[tc - HALO:tc, :3 * width]


def _hgrn_block(p_ref, lbp, gain, o_ref, state_ref, tc, layer, out_col):
    c = HGRN_CHUNK
    hd = HEAD_DIM
    sup = HGRN_SUPER
    width = N_HEADS * hd
    n_sub = sup // c

    lbe = jnp.exp(lbp - jnp.max(lbp, axis=0, keepdims=True))
    lbs = lbe / jnp.sum(lbe, axis=0, keepdims=True)
    lb_all = jnp.sum(lbs[0:layer + 1], axis=0, keepdims=True) - lbs[0:1]

    half = c // 2
    pos = lax.broadcasted_iota(jnp.int32, (sup, hd), 0) % c
    tile_row = lax.broadcasted_iota(jnp.int32, (half, hd), 0)
    n_pair = half * half + half * c
    gi = lax.broadcasted_iota(jnp.int32, (c, n_pair), 0)
    gj = lax.broadcasted_iota(jnp.int32, (c, n_pair), 1)
    owner = jnp.where(gj < half * half, gj // half, half + (gj - half * half) // c)
    group = jnp.broadcast_to((owner == gi).astype(BF16), (n_sub, c, n_pair))
    ones = jnp.ones((hd, hd), BF16)

    for h in range(N_HEADS):
        lb = lb_all[:, h * hd:(h + 1) * hd]
        st = state_ref[h]
        for r0 in range(0, tc, sup):
            col = lambda part: slice(part * width + h * hd, part * width + (h + 1) * hd)
            f = lb + (1.0 - lb) * _sigmoid(p_ref[r0:r0 + sup, col(1)])
            log_f = jnp.log(jnp.maximum(f, F_FLOOR))
            kk = 1.0 - f
            q = _silu(p_ref[r0:r0 + sup, col(0)])
            v = p_ref[r0:r0 + sup, col(2)]
            b = log_f
            tail = log_f
            s = 1
            while s < c:
                b = b + jnp.where(pos >= s, pltpu.roll(b, s, 0), 0.0)
                tail = tail + jnp.where(pos < c - s, pltpu.roll(tail, sup - s, 0), 0.0)
                s *= 2
            e_b = jnp.exp(b)
            q_dec = q * e_b
            k_dec = kk * jnp.exp(tail - log_f)
            pairs, v_reps = [], []
            for sc in range(n_sub):
                lo = sc * c
                b_a, b_b = b[lo:lo + half], b[lo + half:lo + c]
                k_a, k_b = kk[lo:lo + half], kk[lo + half:lo + c]
                v_a = v[lo:lo + half]
                v_c = v[lo:lo + c]
                for i in range(c):
                    b_i = b[lo + i:lo + i + 1]
                    q_i = q[lo + i:lo + i + 1]
                    if i < half:
                        pairs.append(jnp.exp(jnp.where(tile_row <= i, b_i - b_a, -1e30)) * k_a * q_i)
                        v_reps.append(v_a)
                    else:
                        pairs.append(jnp.exp(b_i - b_a) * k_a * q_i)
                        pairs.append(jnp.exp(jnp.where(tile_row <= i - half, b_i - b_b, -1e30))
                                     * k_b * q_i)
                        v_reps.append(v_c)
            scores = _mm(jnp.concatenate(pairs, axis=0), ones)
            weighted = (scores * jnp.concatenate(v_reps, axis=0)).reshape(n_sub, n_pair, hd)
            o_intra = _bmm(group, weighted)
            updates = [_mm_tn(v[sc * c:(sc + 1) * c], k_dec[sc * c:(sc + 1) * c])
                       for sc in range(n_sub)]
            states = [st]
            for sc in range(n_sub):
                states.append(states[sc] * e_b[sc * c + c - 1:sc * c + c] + updates[sc])
            o = jnp.concatenate([o_intra[sc] + _mm_nt(q_dec[sc * c:(sc + 1) * c], states[sc])
                                 for sc in range(n_sub)], axis=0)
            st = states[n_sub]
            gate = _silu(p_ref[r0:r0 + sup, col(3)])
            o_ref[r0:r0 + sup, out_col + h * hd:out_col + (h + 1) * hd] = (
                _rms_rows(o, gain) * gate).astype(BF16)
        state_ref[h] = st


def _even_kernel(x_ref, g_ref, wa_ref, wb_ref, ws_ref, wst_ref, cw_ref, alog_ref, dt_ref,
                 alog_t_ref, dt_t_ref, ggain_ref, lbp_ref, hgain_ref, o_ref,
                 pa_ref, pb_ref, gstate_ref, ghalo_ref, hstate_ref, *, tc, layer):
    @pl.when(pl.program_id(1) == 0)
    def _():
        gstate_ref[...] = jnp.zeros_like(gstate_ref)
        ghalo_ref[...] = jnp.zeros_like(ghalo_ref)
        hstate_ref[...] = jnp.zeros_like(hstate_ref)

    hn = _rms_rows(x_ref[...], g_ref[...]).astype(BF16)
    pa_ref[...] = jnp.dot(hn, wa_ref[...], preferred_element_type=F32)
    pb_ref[...] = jnp.dot(hn, wb_ref[...], preferred_element_type=F32)
    small = jnp.dot(hn, ws_ref[...], preferred_element_type=F32)
    small_t = _mm_nt(wst_ref[...], hn)
    _gdn_block(pa_ref, small, small_t, cw_ref[...], alog_ref[...], dt_ref[...], alog_t_ref[...],
               dt_t_ref[...], ggain_ref[...], o_ref, gstate_ref, ghalo_ref, tc)
    _hgrn_block(pb_ref, lbp_ref[...], hgain_ref[...], o_ref, hstate_ref, tc, layer,
                N_HEADS * HEAD_DIM)


def _split_even_in_proj(w_in):
    split = 4 * N_HEADS * HEAD_DIM
    return (w_in[:, :, :split].astype(BF16), w_in[:, :, split + 2 * N_HEADS:].astype(BF16),
            w_in[:, :, split:split + 2 * N_HEADS])


def _even_mixer(x, gain, w_a, w_b, w_gates, conv_w, a_log, dt_bias, gdn_gain, lower_bounds_raw,
                hgrn_gain, *, layer, batch, seq, tc=512):
    m, d = x.shape
    nt = seq // tc
    width = N_HEADS * HEAD_DIM
    split = 4 * width
    n_t = 2 * HALO
    w_small = jnp.pad(w_gates, ((0, 0), (0, HEAD_DIM - 2 * N_HEADS))).astype(BF16)
    w_small_t = jnp.pad(w_gates.T, ((0, n_t - 2 * N_HEADS), (0, 0))).astype(BF16)
    lane_vec = lambda v: jnp.pad(v.astype(F32), (N_HEADS, HEAD_DIM - 2 * N_HEADS)).reshape(1, HEAD_DIM)
    row_vec = lambda v: jnp.pad(v.astype(F32), (N_HEADS, n_t - 2 * N_HEADS)).reshape(n_t, 1)
    row = lambda b, t: (b * nt + t, 0)
    const = lambda b, t: (0, 0)
    return pl.pallas_call(
        functools.partial(_even_kernel, tc=tc, layer=layer),
        grid=(batch, nt),
        in_specs=[pl.BlockSpec((tc, d), row),
                  pl.BlockSpec((1, d), const),
                  pl.BlockSpec((None, d, split), lambda b, t: (layer, 0, 0), **RESIDENT),
                  pl.BlockSpec((None, d, split), lambda b, t: (layer, 0, 0), **RESIDENT),
                  pl.BlockSpec((d, HEAD_DIM), const),
                  pl.BlockSpec((n_t, d), const),
                  pl.BlockSpec((GDN_CONV, 3 * width), const),
                  pl.BlockSpec((1, HEAD_DIM), const),
                  pl.BlockSpec((1, HEAD_DIM), const),
                  pl.BlockSpec((n_t, 1), const),
                  pl.BlockSpec((n_t, 1), const),
                  pl.BlockSpec((1, HEAD_DIM), const),
                  pl.BlockSpec(lower_bounds_raw.shape, const),
                  pl.BlockSpec((1, HEAD_DIM), const)],
        out_specs=pl.BlockSpec((tc, 2 * width), row),
        out_shape=jax.ShapeDtypeStruct((m, 2 * width), BF16),
        scratch_shapes=[pltpu.VMEM((tc, split), F32),
                        pltpu.VMEM((tc, split), F32),
                        pltpu.VMEM((N_HEADS, HEAD_DIM, HEAD_DIM), F32),
                        pltpu.VMEM((HALO, 3 * width), F32),
                        pltpu.VMEM((N_HEADS, HEAD_DIM, HEAD_DIM), F32)],
        compiler_params=_params("parallel", "arbitrary"),
        name="even_mixer",
    )(x, gain.reshape(1, d), w_a, w_b, w_small, w_small_t, conv_w, lane_vec(a_log),
      lane_vec(dt_bias), row_vec(a_log), row_vec(dt_bias), gdn_gain.reshape(1, HEAD_DIM),
      lower_bounds_raw, hgrn_gain.reshape(1, HEAD_DIM))


def _rglru_kernel(res_ref, g_ref, win_ref, cw_ref, cb_ref, wa_ref, ba_ref, wx_ref, bx_ref,
                  lam_ref, o_ref, h_ref, halo_ref, *, tc):
    n_heads = wa_ref.shape[0]
    blk = LRU_BLOCK
    width = n_heads * blk

    @pl.when(pl.program_id(1) == 0)
    def _():
        h_ref[...] = jnp.zeros_like(h_ref)
        halo_ref[...] = jnp.zeros_like(halo_ref)

    hn = _rms_rows(res_ref[...], g_ref[...]).astype(BF16)
    y = jnp.dot(hn, win_ref[:, :width], preferred_element_type=F32)
    x = jnp.dot(hn, win_ref[:, width:], preferred_element_type=F32)
    ext = jnp.concatenate([halo_ref[...], x], axis=0)
    halo_ref[...] = x[tc - HALO:tc]
    xc = _causal_conv(ext, cw_ref[...], LRU_CONV) + cb_ref[...]
    ra = jnp.concatenate([_mm(xc[:, h * blk:(h + 1) * blk], wa_ref[h]) for h in range(n_heads)],
                         axis=1)
    rx = jnp.concatenate([_mm(xc[:, h * blk:(h + 1) * blk], wx_ref[h]) for h in range(n_heads)],
                         axis=1)
    r = _sigmoid(ra + ba_ref[...])
    gi = _sigmoid(rx + bx_ref[...])
    log_a = r * (-RG_C * _softplus(-lam_ref[...]))
    a = jnp.exp(log_a)
    th = jnp.tanh(log_a)
    one_minus_a2 = -2.0 * th / (1.0 - th)
    u = jnp.sqrt(jnp.maximum(one_minus_a2, 0.0)) * (gi * xc)

    first = lax.broadcasted_iota(jnp.int32, (HALO, width), 0) == 0
    carry = h_ref[0:1, :]
    tiles = []
    for t in range(tc // HALO):
        a_t = a[t * HALO:(t + 1) * HALO]
        ht = u[t * HALO:(t + 1) * HALO] + jnp.where(first, a_t * carry, 0.0)
        mult = jnp.where(first, 0.0, a_t)
        for s in (1, 2, 4):
            ht = ht + mult * pltpu.roll(ht, s, 0)
            if s < HALO // 2:
                mult = mult * pltpu.roll(mult, s, 0)
        tiles.append(ht)
        carry = ht[HALO - 1:HALO]
    h_ref[...] = jnp.broadcast_to(carry, h_ref.shape)
    hs = jnp.concatenate(tiles, axis=0)
    gate = 0.5 * y * (1.0 + jnp.tanh(0.7978845608028654 * (y + 0.044715 * (y * y * y))))
    o_ref[...] = (hs * gate).astype(BF16)


def _rglru(x, gain, w_in, conv_w, conv_b, wa, ba, wx, bx, lam, *, layer, batch, seq, tc=512):
    m, d = x.shape
    nt = seq // tc
    width = conv_w.shape[1]
    const = lambda b, t: (0, 0)
    row = lambda b, t: (b * nt + t, 0)
    vec = pl.BlockSpec((1, width), const)
    mat = pl.BlockSpec((None,) + wa.shape[1:], lambda b, t: (layer, 0, 0, 0))
    return pl.pallas_call(
        functools.partial(_rglru_kernel, tc=tc),
        grid=(batch, nt),
        in_specs=[pl.BlockSpec((tc, d), row),
                  pl.BlockSpec((1, d), const),
                  pl.BlockSpec((None,) + w_in.shape[1:], lambda b, t: (layer, 0, 0), **RESIDENT),
                  pl.BlockSpec((LRU_CONV, width), const),
                  vec, mat, vec, mat, vec, vec],
        out_specs=pl.BlockSpec((tc, width), row),
        out_shape=jax.ShapeDtypeStruct((m, width), BF16),
        scratch_shapes=[pltpu.VMEM((HALO, width), F32), pltpu.VMEM((HALO, width), F32)],
        compiler_params=_params("parallel", "arbitrary"),
        name="rglru",
    )(x, gain.reshape(1, d), w_in, conv_w, conv_b.reshape(1, width), wa, ba.reshape(1, width),
      wx, bx.reshape(1, width), lam.reshape(1, width))


def _ffn_kernel(x_ref, mix_ref, wmix_ref, g_ref, wu_ref, cw_ref, cb_ref, wd_ref, gf_ref, o_ref,
                xm_ref, h_ref, act_ref, halo_ref, *, tiles_per_seq, tf, final_norm):
    d_ff = wd_ref.shape[0]

    @pl.when(pl.program_id(0) % tiles_per_seq == 0)
    def _():
        halo_ref[...] = jnp.zeros_like(halo_ref)

    xm = x_ref[...] + jnp.dot(mix_ref[...], wmix_ref[...], preferred_element_type=F32)
    xm_ref[...] = xm
    h_ref[...] = _rms_rows(xm, g_ref[...]).astype(BF16)
    for c0 in range(0, d_ff, tf):
        h = h_ref[...]
        gate = jnp.dot(h, wu_ref[:, c0:c0 + tf], preferred_element_type=F32)
        val = jnp.dot(h, wu_ref[:, d_ff + c0:d_ff + c0 + tf], preferred_element_type=F32)
        ext = jnp.concatenate([halo_ref[:, c0:c0 + tf], gate], axis=0)
        halo_ref[:, c0:c0 + tf] = gate[gate.shape[0] - HALO:]
        gc = _causal_conv(ext, cw_ref[:, c0:c0 + tf], FFN_CONV) + cb_ref[:, c0:c0 + tf]
        act_ref[:, c0:c0 + tf] = (_silu(gc) * val).astype(BF16)
    acc = xm_ref[...] + jnp.dot(act_ref[...], wd_ref[...], preferred_element_type=F32)
    if final_norm:
        acc = _rms_rows(acc, gf_ref[...])
    o_ref[...] = acc


def _mix_ffn(x, mix, w_mix, mix_layer, gain, w_up, conv_w, conv_b, w_down, final_gain, *, layer,
             seq, final_norm, tm=1024, tf=256):
    m, d = x.shape
    d_ff = w_down.shape[1]
    row = lambda i: (i, 0)
    const = lambda i: (0, 0)
    return pl.pallas_call(
        functools.partial(_ffn_kernel, tiles_per_seq=seq // tm, tf=tf, final_norm=final_norm),
        grid=(m // tm,),
        in_specs=[pl.BlockSpec((tm, d), row),
                  pl.BlockSpec((tm, mix.shape[1]), row),
                  pl.BlockSpec((None,) + w_mix.shape[1:], lambda i: (mix_layer, 0, 0), **RESIDENT),
                  pl.BlockSpec((1, d), const),
                  pl.BlockSpec((None, d, 2 * d_ff), lambda i: (layer, 0, 0), **RESIDENT),
                  pl.BlockSpec((FFN_CONV, d_ff), const),
                  pl.BlockSpec((1, d_ff), const),
                  pl.BlockSpec((None, d_ff, d), lambda i: (layer, 0, 0), **RESIDENT),
                  pl.BlockSpec((1, d), const)],
        out_specs=pl.BlockSpec((tm, d), row),
        out_shape=jax.ShapeDtypeStruct((m, d), F32),
        scratch_shapes=[pltpu.VMEM((tm, d), F32),
                        pltpu.VMEM((tm, d), BF16),
                        pltpu.VMEM((tm, d_ff), BF16),
                        pltpu.VMEM((HALO, d_ff), F32)],
        compiler_params=_params("arbitrary"),
        name="mix_ffn",
    )(x, mix, w_mix, gain.reshape(1, d), w_up, conv_w, conv_b.reshape(1, d_ff), w_down,
      final_gain.reshape(1, d))


def kernel(x, norm_mix, norm_ffn, norm_final, ab_w_in, gdn_conv_w, gdn_a_log, gdn_dt_bias,
           gdn_norm, hgrn_lower_bounds, hgrn_norm, ab_w_out, c_w_in, c_conv_w, c_conv_b,
           c_gate_a_w, c_gate_a_b, c_gate_x_w, c_gate_x_b, c_lambda, c_w_out,
           ffn_w_up, ffn_conv_w, ffn_conv_b, ffn_w_down):
    batch, seq, d = x.shape
    depth = norm_mix.shape[0]
    x = x.reshape(batch * seq, d)
    ab_w_a, ab_w_b, ab_w_gates = _split_even_in_proj(ab_w_in)
    ab_w_out, c_w_in, c_w_out = (w.astype(BF16) for w in (ab_w_out, c_w_in, c_w_out))
    c_gate_a_w, c_gate_x_w = c_gate_a_w.astype(BF16), c_gate_x_w.astype(BF16)
    ffn_w_up, ffn_w_down = ffn_w_up.astype(BF16), ffn_w_down.astype(BF16)
    for layer in range(depth):
        j = layer // 2
        if layer % 2 == 0:
            mix = _even_mixer(x, norm_mix[layer], ab_w_a, ab_w_b, ab_w_gates[j], gdn_conv_w[j],
                              gdn_a_log[j], gdn_dt_bias[j], gdn_norm[j], hgrn_lower_bounds,
                              hgrn_norm[j], layer=j, batch=batch, seq=seq)
            w_mix = ab_w_out
        else:
            mix = _rglru(x, norm_mix[layer], c_w_in, c_conv_w[j], c_conv_b[j], c_gate_a_w,
                         c_gate_a_b[j], c_gate_x_w, c_gate_x_b[j], c_lambda[j], layer=j,
                         batch=batch, seq=seq)
            w_mix = c_w_out
        x = _mix_ffn(x, mix, w_mix, j, norm_ffn[layer], ffn_w_up, ffn_conv_w[layer],
                     ffn_conv_b[layer], ffn_w_down, norm_final, layer=layer, seq=seq,
                     final_norm=layer == depth - 1)
    return x.reshape(batch, seq, d)
```

```python
import functools

import jax
import jax.numpy as jnp
from jax import lax
from jax.experimental import pallas as pl
from jax.experimental.pallas import tpu as pltpu

F32 = jnp.float32
BF16 = jnp.bfloat16

EPS = 1e-6
F_FLOOR = 1e-30
RG_C = 8.0

HEAD_DIM = 128
N_HEADS = 4
GDN_CHUNK = 64
GDN_CONV = 4
HGRN_CHUNK = 16
HGRN_SUPER = 256
LRU_BLOCK = 256
LRU_CONV = 4
FFN_CONV = 3
HALO = 8

VMEM_LIMIT_BYTES = 56 * 1024 * 1024
RESIDENT = dict(pipeline_mode=pl.Buffered(1))


def _params(*semantics):
    return pltpu.CompilerParams(dimension_semantics=semantics,
                                vmem_limit_bytes=VMEM_LIMIT_BYTES)


def _sigmoid(x):
    return 1.0 / (1.0 + jnp.exp(-x))


def _silu(x):
    return x * _sigmoid(x)


def _softplus(x):
    return jnp.maximum(x, 0.0) + jnp.log(1.0 + jnp.exp(-jnp.abs(x)))


def _mm(a, b):
    return jnp.dot(a.astype(BF16), b.astype(BF16), preferred_element_type=F32)


def _mm_nt(a, b):
    return lax.dot_general(a.astype(BF16), b.astype(BF16), (((1,), (1,)), ((), ())),
                           preferred_element_type=F32)


def _mm_tn(a, b):
    return lax.dot_general(a.astype(BF16), b.astype(BF16), (((0,), (0,)), ((), ())),
                           preferred_element_type=F32)


def _bmm(a, b):
    return lax.dot_general(a.astype(BF16), b.astype(BF16), (((2,), (1,)), ((0,), (0,))),
                           preferred_element_type=F32)


def _bmm_nt(a, b):
    return lax.dot_general(a.astype(BF16), b.astype(BF16), (((2,), (2,)), ((0,), (0,))),
                           preferred_element_type=F32)


def _rms_rows(x, gain):
    ms = jnp.mean(x * x, axis=-1, keepdims=True)
    return x * lax.rsqrt(ms + EPS) * gain


def _causal_conv(ext, w, width):
    out = ext[HALO:] * w[width - 1:width]
    for k in range(width - 1):
        shift = width - 1 - k
        out = out + pltpu.roll(ext, shift, 0)[HALO:] * w[k:k + 1]
    return out


def _inv_unit_lower(low, ii, jj):
    eye = (ii == jj).astype(F32)
    same_block = (ii // 16) == (jj // 16)
    d1 = jnp.where(same_block, low, 0.0)
    e = jnp.where(same_block, 0.0, low)
    d2 = _bmm(d1, d1)
    d4 = _bmm(d2, d2)
    d8 = _bmm(d4, d4)
    dinv = _bmm(eye - d1, eye + d2)
    dinv = _bmm(dinv, eye + d4)
    dinv = _bmm(dinv, eye + d8)
    n1 = _bmm(dinv, e)
    n2 = _bmm(n1, n1)
    return _bmm(_bmm(eye - n1, eye + n2), dinv)


def _chunk_cumsum(x, axis):
    pos = lax.broadcasted_iota(jnp.int32, x.shape, axis) % GDN_CHUNK
    s = 1
    while s < GDN_CHUNK:
        x = x + jnp.where(pos >= s, pltpu.roll(x, s, axis), 0.0)
        s *= 2
    return x


def _gdn_block(p_ref, small, small_t, cw, alog, dt, alog_t, dt_t, gain, o_ref, state_ref,
               halo_ref, tc):
    c = GDN_CHUNK
    hd = HEAD_DIM
    width = N_HEADS * hd
    ii = lax.broadcasted_iota(jnp.int32, (c, c), 0)
    jj = lax.broadcasted_iota(jnp.int32, (c, c), 1)
    causal = ii >= jj
    strict = ii > jj

    beta_all = _sigmoid(small)
    gc_all = _chunk_cumsum(-jnp.exp(alog) * _softplus(small + dt), 0)
    gc_t = _chunk_cumsum(-jnp.exp(alog_t) * _softplus(small_t + dt_t), 1)

    qs, ks, kbs, decays, rhss, qds, kds, carries = [], [], [], [], [], [], [], []
    for ci in range(tc // c):
        r0 = ci * c
        prev = halo_ref[...] if ci == 0 else p_ref[r0 - HALO:r0, :3 * width]
        ext = jnp.concatenate([prev, p_ref[r0:r0 + c, :3 * width]], axis=0)
        qkv = _silu(_causal_conv(ext, cw, GDN_CONV))
        for h in range(N_HEADS):
            q = qkv[:, h * hd:(h + 1) * hd]
            k = qkv[:, width + h * hd:width + (h + 1) * hd]
            v = qkv[:, 2 * width + h * hd:2 * width + (h + 1) * hd]
            q = q * lax.rsqrt(jnp.sum(q * q, axis=-1, keepdims=True) + EPS) * (hd ** -0.5)
            k = k * lax.rsqrt(jnp.sum(k * k, axis=-1, keepdims=True) + EPS)
            beta = beta_all[r0:r0 + c, h:h + 1]
            gc_col = gc_all[r0:r0 + c, N_HEADS + h:N_HEADS + h + 1]
            gc_row = gc_t[N_HEADS + h:N_HEADS + h + 1, r0:r0 + c]
            exp_gc = jnp.exp(gc_col)
            gc_last = gc_col[c - 1:c, :]
            k_beta = k * beta
            qs.append(q)
            ks.append(k)
            kbs.append(k_beta)
            decays.append(jnp.where(causal, jnp.exp(jnp.where(causal, gc_col - gc_row, 0.0)), 0.0))
            rhss.append(jnp.concatenate([v * beta, k_beta * exp_gc], axis=1))
            qds.append(q * exp_gc)
            kds.append(k * jnp.exp(gc_last - gc_col))
            carries.append(jnp.exp(gc_last))
    q_all, k_all, decay = jnp.stack(qs), jnp.stack(ks), jnp.stack(decays)
    low = jnp.where(strict, _bmm_nt(jnp.stack(kbs), k_all) * decay, 0.0)
    uw = _bmm(_inv_unit_lower(low, ii, jj), jnp.stack(rhss))
    attn = _bmm_nt(q_all, k_all) * decay

    state = state_ref[...]
    for ci in range(tc // c):
        r0 = ci * c
        sl = slice(ci * N_HEADS, (ci + 1) * N_HEADS)
        wq = _bmm(jnp.concatenate([uw[sl, :, hd:], jnp.stack(qds[sl])], axis=1), state)
        v_new = uw[sl, :, :hd] - wq[:, :c]
        o = wq[:, c:] + _bmm(attn[sl], v_new)
        state = jnp.stack([state[h] * carries[ci * N_HEADS + h]
                           + _mm_tn(kds[ci * N_HEADS + h], v_new[h]) for h in range(N_HEADS)])
        for h in range(N_HEADS):
            zg = _silu(p_ref[r0:r0 + c, 3 * width + h * hd:3 * width + (h + 1) * hd])
            o_ref[r0:r0 + c, h * hd:(h + 1) * hd] = (_rms_rows(o[h], gain) * zg).astype(BF16)
    state_ref[...] = state
    halo_ref[...] = p_ref[tc - HALO:tc, :3 * width]


def _hgrn_block(p_ref, lbp, gain, o_ref, state_ref, tc, layer, out_col):
    c = HGRN_CHUNK
    hd = HEAD_DIM
    sup = HGRN_SUPER
    width = N_HEADS * hd
    n_sub = sup // c

    lbe = jnp.exp(lbp - jnp.max(lbp, axis=0, keepdims=True))
    lbs = lbe / jnp.sum(lbe, axis=0, keepdims=True)
    lb_all = jnp.sum(lbs[0:layer + 1], axis=0, keepdims=True) - lbs[0:1]

    half = c // 2
    pos = lax.broadcasted_iota(jnp.int32, (sup, hd), 0) % c
    tile_row = lax.broadcasted_iota(jnp.int32, (half, hd), 0)
    n_pair = half * half + half * c
    gi = lax.broadcasted_iota(jnp.int32, (c, n_pair), 0)
    gj = lax.broadcasted_iota(jnp.int32, (c, n_pair), 1)
    owner = jnp.where(gj < half * half, gj // half, half + (gj - half * half) // c)
    group = jnp.broadcast_to((owner == gi).astype(BF16), (n_sub, c, n_pair))
    ones = jnp.ones((hd, hd), BF16)

    for h in range(N_HEADS):
        lb = lb_all[:, h * hd:(h + 1) * hd]
        st = state_ref[h]
        for r0 in range(0, tc, sup):
            col = lambda part: slice(part * width + h * hd, part * width + (h + 1) * hd)
            f = lb + (1.0 - lb) * _sigmoid(p_ref[r0:r0 + sup, col(1)])
            log_f = jnp.log(jnp.maximum(f, F_FLOOR))
            kk = 1.0 - f
            q = _silu(p_ref[r0:r0 + sup, col(0)])
            v = p_ref[r0:r0 + sup, col(2)]
            b = log_f
            tail = log_f
            s = 1
            while s < c:
                b = b + jnp.where(pos >= s, pltpu.roll(b, s, 0), 0.0)
                tail = tail + jnp.where(pos < c - s, pltpu.roll(tail, sup - s, 0), 0.0)
                s *= 2
            e_b = jnp.exp(b)
            q_dec = q * e_b
            k_dec = kk * jnp.exp(tail - log_f)
            pairs, v_reps = [], []
            for sc in range(n_sub):
                lo = sc * c
                b_a, b_b = b[lo:lo + half], b[lo + half:lo + c]
                k_a, k_b = kk[lo:lo + half], kk[lo + half:lo + c]
                v_a = v[lo:lo + half]
                v_c = v[lo:lo + c]
                for i in range(c):
                    b_i = b[lo + i:lo + i + 1]
                    q_i = q[lo + i:lo + i + 1]
                    if i < half:
                        pairs.append(jnp.exp(jnp.where(tile_row <= i, b_i - b_a, -1e30)) * k_a * q_i)
                        v_reps.append(v_a)
                    else:
                        pairs.append(jnp.exp(b_i - b_a) * k_a * q_i)
                        pairs.append(jnp.exp(jnp.where(tile_row <= i - half, b_i - b_b, -1e30))
                                     * k_b * q_i)
                        v_reps.append(v_c)
            scores = _mm(jnp.concatenate(pairs, axis=0), ones)
            weighted = (scores * jnp.concatenate(v_reps, axis=0)).reshape(n_sub, n_pair, hd)
            o_intra = _bmm(group, weighted)
            updates = [_mm_tn(v[sc * c:(sc + 1) * c], k_dec[sc * c:(sc + 1) * c])
                       for sc in range(n_sub)]
            states = [st]
            for sc in range(n_sub):
                states.append(states[sc] * e_b[sc * c + c - 1:sc * c + c] + updates[sc])
            o = jnp.concatenate([o_intra[sc] + _mm_nt(q_dec[sc * c:(sc + 1) * c], states[sc])
                                 for sc in range(n_sub)], axis=0)
            st = states[n_sub]
            gate = _silu(p_ref[r0:r0 + sup, col(3)])
            o_ref[r0:r0 + sup, out_col + h * hd:out_col + (h + 1) * hd] = (
                _rms_rows(o, gain) * gate).astype(BF16)
        state_ref[h] = st


def _even_kernel(x_ref, g_ref, wa_ref, wb_ref, ws_ref, wst_ref, cw_ref, alog_ref, dt_ref,
                 alog_t_ref, dt_t_ref, ggain_ref, lbp_ref, hgain_ref, o_ref,
                 pa_ref, pb_ref, gstate_ref, ghalo_ref, hstate_ref, *, tc, layer):
    @pl.when(pl.program_id(1) == 0)
    def _():
        gstate_ref[...] = jnp.zeros_like(gstate_ref)
        ghalo_ref[...] = jnp.zeros_like(ghalo_ref)
        hstate_ref[...] = jnp.zeros_like(hstate_ref)

    hn = _rms_rows(x_ref[...], g_ref[...]).astype(BF16)
    pa_ref[...] = jnp.dot(hn, wa_ref[...], preferred_element_type=F32)
    pb_ref[...] = jnp.dot(hn, wb_ref[...], preferred_element_type=F32)
    small = jnp.dot(hn, ws_ref[...], preferred_element_type=F32)
    small_t = _mm_nt(wst_ref[...], hn)
    _gdn_block(pa_ref, small, small_t, cw_ref[...], alog_ref[...], dt_ref[...], alog_t_ref[...],
               dt_t_ref[...], ggain_ref[...], o_ref, gstate_ref, ghalo_ref, tc)
    _hgrn_block(pb_ref, lbp_ref[...], hgain_ref[...], o_ref, hstate_ref, tc, layer,
                N_HEADS * HEAD_DIM)


def _split_even_in_proj(w_in):
    split = 4 * N_HEADS * HEAD_DIM
    return (w_in[:, :, :split].astype(BF16), w_in[:, :, split + 2 * N_HEADS:].astype(BF16),
            w_in[:, :, split:split + 2 * N_HEADS])


def _even_mixer(x, gain, w_a, w_b, w_gates, conv_w, a_log, dt_bias, gdn_gain, lower_bounds_raw,
                hgrn_gain, *, layer, batch, seq, tc=512):
    m, d = x.shape
    nt = seq // tc
    width = N_HEADS * HEAD_DIM
    split = 4 * width
    n_t = 2 * HALO
    w_small = jnp.pad(w_gates, ((0, 0), (0, HEAD_DIM - 2 * N_HEADS))).astype(BF16)
    w_small_t = jnp.pad(w_gates.T, ((0, n_t - 2 * N_HEADS), (0, 0))).astype(BF16)
    lane_vec = lambda v: jnp.pad(v.astype(F32), (N_HEADS, HEAD_DIM - 2 * N_HEADS)).reshape(1, HEAD_DIM)
    row_vec = lambda v: jnp.pad(v.astype(F32), (N_HEADS, n_t - 2 * N_HEADS)).reshape(n_t, 1)
    row = lambda b, t: (b * nt + t, 0)
    const = lambda b, t: (0, 0)
    return pl.pallas_call(
        functools.partial(_even_kernel, tc=tc, layer=layer),
        grid=(batch, nt),
        in_specs=[pl.BlockSpec((tc, d), row),
                  pl.BlockSpec((1, d), const),
                  pl.BlockSpec((None, d, split), lambda b, t: (layer, 0, 0), **RESIDENT),
                  pl.BlockSpec((None, d, split), lambda b, t: (layer, 0, 0), **RESIDENT),
                  pl.BlockSpec((d, HEAD_DIM), const),
                  pl.BlockSpec((n_t, d), const),
                  pl.BlockSpec((GDN_CONV, 3 * width), const),
                  pl.BlockSpec((1, HEAD_DIM), const),
                  pl.BlockSpec((1, HEAD_DIM), const),
                  pl.BlockSpec((n_t, 1), const),
                  pl.BlockSpec((n_t, 1), const),
                  pl.BlockSpec((1, HEAD_DIM), const),
                  pl.BlockSpec(lower_bounds_raw.shape, const),
                  pl.BlockSpec((1, HEAD_DIM), const)],
        out_specs=pl.BlockSpec((tc, 2 * width), row),
        out_shape=jax.ShapeDtypeStruct((m, 2 * width), BF16),
        scratch_shapes=[pltpu.VMEM((tc, split), F32),
                        pltpu.VMEM((tc, split), F32),
                        pltpu.VMEM((N_HEADS, HEAD_DIM, HEAD_DIM), F32),
                        pltpu.VMEM((HALO, 3 * width), F32),
                        pltpu.VMEM((N_HEADS, HEAD_DIM, HEAD_DIM), F32)],
        compiler_params=_params("parallel", "arbitrary"),
        name="even_mixer",
    )(x, gain.reshape(1, d), w_a, w_b, w_small, w_small_t, conv_w, lane_vec(a_log),
      lane_vec(dt_bias), row_vec(a_log), row_vec(dt_bias), gdn_gain.reshape(1, HEAD_DIM),
      lower_bounds_raw, hgrn_gain.reshape(1, HEAD_DIM))


def _rglru_kernel(res_ref, g_ref, win_ref, cw_ref, cb_ref, wa_ref, ba_ref, wx_ref, bx_ref,
                  lam_ref, o_ref, h_ref, halo_ref, *, tc):
    n_heads = wa_ref.shape[0]
    blk = LRU_BLOCK
    width = n_heads * blk

    @pl.when(pl.program_id(1) == 0)
    def _():
        h_ref[...] = jnp.zeros_like(h_ref)
        halo_ref[...] = jnp.zeros_like(halo_ref)

    hn = _rms_rows(res_ref[...], g_ref[...]).astype(BF16)
    y = jnp.dot(hn, win_ref[:, :width], preferred_element_type=F32)
    x = jnp.dot(hn, win_ref[:, width:], preferred_element_type=F32)
    ext = jnp.concatenate([halo_ref[...], x], axis=0)
    halo_ref[...] = x[tc - HALO:tc]
    xc = _causal_conv(ext, cw_ref[...], LRU_CONV) + cb_ref[...]
    ra = jnp.concatenate([_mm(xc[:, h * blk:(h + 1) * blk], wa_ref[h]) for h in range(n_heads)],
                         axis=1)
    rx = jnp.concatenate([_mm(xc[:, h * blk:(h + 1) * blk], wx_ref[h]) for h in range(n_heads)],
                         axis=1)
    r = _sigmoid(ra + ba_ref[...])
    gi = _sigmoid(rx + bx_ref[...])
    log_a = r * (-RG_C * _softplus(-lam_ref[...]))
    a = jnp.exp(log_a)
    th = jnp.tanh(log_a)
    one_minus_a2 = -2.0 * th / (1.0 - th)
    u = jnp.sqrt(jnp.maximum(one_minus_a2, 0.0)) * (gi * xc)

    first = lax.broadcasted_iota(jnp.int32, (HALO, width), 0) == 0
    carry = h_ref[0:1, :]
    tiles = []
    for t in range(tc // HALO):
        a_t = a[t * HALO:(t + 1) * HALO]
        ht = u[t * HALO:(t + 1) * HALO] + jnp.where(first, a_t * carry, 0.0)
        mult = jnp.where(first, 0.0, a_t)
        for s in (1, 2, 4):
            ht = ht + mult * pltpu.roll(ht, s, 0)
            if s < HALO // 2:
                mult = mult * pltpu.roll(mult, s, 0)
        tiles.append(ht)
        carry = ht[HALO - 1:HALO]
    h_ref[...] = jnp.broadcast_to(carry, h_ref.shape)
    hs = jnp.concatenate(tiles, axis=0)
    gate = 0.5 * y * (1.0 + jnp.tanh(0.7978845608028654 * (y + 0.044715 * (y * y * y))))
    o_ref[...] = (hs * gate).astype(BF16)


def _rglru(x, gain, w_in, conv_w, conv_b, wa, ba, wx, bx, lam, *, layer, batch, seq, tc=512):
    m, d = x.shape
    nt = seq // tc
    width = conv_w.shape[1]
    const = lambda b, t: (0, 0)
    row = lambda b, t: (b * nt + t, 0)
    vec = pl.BlockSpec((1, width), const)
    mat = pl.BlockSpec((None,) + wa.shape[1:], lambda b, t: (layer, 0, 0, 0))
    return pl.pallas_call(
        functools.partial(_rglru_kernel, tc=tc),
        grid=(batch, nt),
        in_specs=[pl.BlockSpec((tc, d), row),
                  pl.BlockSpec((1, d), const),
                  pl.BlockSpec((None,) + w_in.shape[1:], lambda b, t: (layer, 0, 0), **RESIDENT),
                  pl.BlockSpec((LRU_CONV, width), const),
                  vec, mat, vec, mat, vec, vec],
        out_specs=pl.BlockSpec((tc, width), row),
        out_shape=jax.ShapeDtypeStruct((m, width), BF16),
        scratch_shapes=[pltpu.VMEM((HALO, width), F32), pltpu.VMEM((HALO, width), F32)],
        compiler_params=_params("parallel", "arbitrary"),
        name="rglru",
    )(x, gain.reshape(1, d), w_in, conv_w, conv_b.reshape(1, width), wa, ba.reshape(1, width),
      wx, bx.reshape(1, width), lam.reshape(1, width))


def _ffn_kernel(x_ref, mix_ref, wmix_ref, g_ref, wu_ref, cw_ref, cb_ref, wd_ref, gf_ref, o_ref,
                xm_ref, h_ref, act_ref, halo_ref, *, tiles_per_seq, tf, final_norm):
    d_ff = wd_ref.shape[0]

    @pl.when(pl.program_id(0) % tiles_per_seq == 0)
    def _():
        halo_ref[...] = jnp.zeros_like(halo_ref)

    xm = x_ref[...] + jnp.dot(mix_ref[...], wmix_ref[...], preferred_element_type=F32)
    xm_ref[...] = xm
    h_ref[...] = _rms_rows(xm, g_ref[...]).astype(BF16)
    for c0 in range(0, d_ff, tf):
        c1 = min(c0 + tf, d_ff)
        h = h_ref[...]
        gate = jnp.dot(h, wu_ref[:, c0:c1], preferred_element_type=F32)
        val = jnp.dot(h, wu_ref[:, d_ff + c0:d_ff + c1], preferred_element_type=F32)
        ext = jnp.concatenate([halo_ref[:, c0:c1], gate], axis=0)
        halo_ref[:, c0:c1] = gate[gate.shape[0] - HALO:]
        gc = _causal_conv(ext, cw_ref[:, c0:c1], FFN_CONV) + cb_ref[:, c0:c1]
        act_ref[:, c0:c1] = (_silu(gc) * val).astype(BF16)
    acc = xm_ref[...] + jnp.dot(act_ref[...], wd_ref[...], preferred_element_type=F32)
    if final_norm:
        acc = _rms_rows(acc, gf_ref[...])
    o_ref[...] = acc


def _mix_ffn(x, mix, w_mix, mix_layer, gain, w_up, conv_w, conv_b, w_down, final_gain, *, layer,
             seq, final_norm, tm=1024, tf=1024):
    m, d = x.shape
    d_ff = w_down.shape[1]
    row = lambda i: (i, 0)
    const = lambda i: (0, 0)
    return pl.pallas_call(
        functools.partial(_ffn_kernel, tiles_per_seq=seq // tm, tf=tf, final_norm=final_norm),
        grid=(m // tm,),
        in_specs=[pl.BlockSpec((tm, d), row),
                  pl.BlockSpec((tm, mix.shape[1]), row),
                  pl.BlockSpec((None,) + w_mix.shape[1:], lambda i: (mix_layer, 0, 0), **RESIDENT),
                  pl.BlockSpec((1, d), const),
                  pl.BlockSpec((None, d, 2 * d_ff), lambda i: (layer, 0, 0), **RESIDENT),
                  pl.BlockSpec((FFN_CONV, d_ff), const),
                  pl.BlockSpec((1, d_ff), const),
                  pl.BlockSpec((None, d_ff, d), lambda i: (layer, 0, 0), **RESIDENT),
                  pl.BlockSpec((1, d), const)],
        out_specs=pl.BlockSpec((tm, d), row),
        out_shape=jax.ShapeDtypeStruct((m, d), F32),
        scratch_shapes=[pltpu.VMEM((tm, d), F32),
                        pltpu.VMEM((tm, d), BF16),
                        pltpu.VMEM((tm, d_ff), BF16),
                        pltpu.VMEM((HALO, d_ff), F32)],
        compiler_params=_params("arbitrary"),
        name="mix_ffn",
    )(x, mix, w_mix, gain.reshape(1, d), w_up, conv_w, conv_b.reshape(1, d_ff), w_down,
      final_gain.reshape(1, d))


def kernel(x, norm_mix, norm_ffn, norm_final, ab_w_in, gdn_conv_w, gdn_a_log, gdn_dt_bias,
           gdn_norm, hgrn_lower_bounds, hgrn_norm, ab_w_out, c_w_in, c_conv_w, c_conv_b,
           c_gate_a_w, c_gate_a_b, c_gate_x_w, c_gate_x_b, c_lambda, c_w_out,
           ffn_w_up, ffn_conv_w, ffn_conv_b, ffn_w_down):
    batch, seq, d = x.shape
    depth = norm_mix.shape[0]
    x = x.reshape(batch * seq, d)
    ab_w_a, ab_w_b, ab_w_gates = _split_even_in_proj(ab_w_in)
    ab_w_out, c_w_in, c_w_out = (w.astype(BF16) for w in (ab_w_out, c_w_in, c_w_out))
    c_gate_a_w, c_gate_x_w = c_gate_a_w.astype(BF16), c_gate_x_w.astype(BF16)
    ffn_w_up, ffn_w_down = ffn_w_up.astype(BF16), ffn_w_down.astype(BF16)
    for layer in range(depth):
        j = layer // 2
        if layer % 2 == 0:
            mix = _even_mixer(x, norm_mix[layer], ab_w_a, ab_w_b, ab_w_gates[j], gdn_conv_w[j],
                              gdn_a_log[j], gdn_dt_bias[j], gdn_norm[j], hgrn_lower_bounds,
                              hgrn_norm[j], layer=j, batch=batch, seq=seq)
            w_mix = ab_w_out
        else:
            mix = _rglru(x, norm_mix[layer], c_w_in, c_conv_w[j], c_conv_b[j], c_gate_a_w,
                         c_gate_a_b[j], c_gate_x_w, c_gate_x_b[j], c_lambda[j], layer=j,
                         batch=batch, seq=seq)
            w_mix = c_w_out
        x = _mix_ffn(x, mix, w_mix, j, norm_ffn[layer], ffn_w_up, ffn_conv_w[layer],
                     ffn_conv_b[layer], ffn_w_down, norm_final, layer=layer, seq=seq,
                     final_norm=layer == depth - 1)
    return x.reshape(batch, seq, d)
```

```python
import functools

import jax
import jax.numpy as jnp
from jax import lax
from jax.experimental import pallas as pl
from jax.experimental.pallas import tpu as pltpu

F32 = jnp.float32
BF16 = jnp.bfloat16

EPS = 1e-6
F_FLOOR = 1e-30
RG_C = 8.0
GELU_C = 0.7978845608028654

HEAD_DIM = 128
N_HEADS = 4
GDN_CHUNK = 64
GDN_CONV = 4
HGRN_CHUNK = 16
HGRN_SUPER = 256
LRU_BLOCK = 256
LRU_CONV = 4
FFN_CONV = 3
HALO = 8

VMEM_LIMIT_BYTES = 56 * 1024 * 1024
RESIDENT = dict(pipeline_mode=pl.Buffered(1))


def _params(*semantics):
    return pltpu.CompilerParams(dimension_semantics=semantics,
                                vmem_limit_bytes=VMEM_LIMIT_BYTES)


def _sigmoid(x):
    return 1.0 / (1.0 + jnp.exp(-x))


def _silu(x):
    return x * _sigmoid(x)


def _softplus(x):
    return jnp.maximum(x, 0.0) + jnp.log(1.0 + jnp.exp(-jnp.abs(x)))


def _mm(a, b):
    return jnp.dot(a.astype(BF16), b.astype(BF16), preferred_element_type=F32)


def _mm_nt(a, b):
    return lax.dot_general(a.astype(BF16), b.astype(BF16), (((1,), (1,)), ((), ())),
                           preferred_element_type=F32)


def _mm_tn(a, b):
    return lax.dot_general(a.astype(BF16), b.astype(BF16), (((0,), (0,)), ((), ())),
                           preferred_element_type=F32)


def _bmm(a, b):
    return lax.dot_general(a.astype(BF16), b.astype(BF16), (((2,), (1,)), ((0,), (0,))),
                           preferred_element_type=F32)


def _bmm_nt(a, b):
    return lax.dot_general(a.astype(BF16), b.astype(BF16), (((2,), (2,)), ((0,), (0,))),
                           preferred_element_type=F32)


def _rms_rows(x, gain):
    ms = jnp.mean(x * x, axis=-1, keepdims=True)
    return x * lax.rsqrt(ms + EPS) * gain


def _causal_conv(ext, w, width):
    out = ext[HALO:] * w[width - 1:width]
    for k in range(width - 1):
        shift = width - 1 - k
        out = out + pltpu.roll(ext, shift, 0)[HALO:] * w[k:k + 1]
    return out


def _inv_unit_lower(low, ii, jj):
    eye = (ii == jj).astype(F32)
    same_block = (ii // 16) == (jj // 16)
    d1 = jnp.where(same_block, low, 0.0)
    e = jnp.where(same_block, 0.0, low)
    d2 = _bmm(d1, d1)
    d4 = _bmm(d2, d2)
    d8 = _bmm(d4, d4)
    dinv = _bmm(eye - d1, eye + d2)
    dinv = _bmm(dinv, eye + d4)
    dinv = _bmm(dinv, eye + d8)
    n1 = _bmm(dinv, e)
    n2 = _bmm(n1, n1)
    return _bmm(_bmm(eye - n1, eye + n2), dinv)


def _chunk_cumsum(x, axis):
    pos = lax.broadcasted_iota(jnp.int32, x.shape, axis) % GDN_CHUNK
    s = 1
    while s < GDN_CHUNK:
        x = x + jnp.where(pos >= s, pltpu.roll(x, s, axis), 0.0)
        s *= 2
    return x


def _gdn_block(p_ref, small, small_t, cw, alog, dt, alog_t, dt_t, gain, o_ref, state_ref,
               halo_ref, tc):
    c = GDN_CHUNK
    hd = HEAD_DIM
    width = N_HEADS * hd
    ii = lax.broadcasted_iota(jnp.int32, (c, c), 0)
    jj = lax.broadcasted_iota(jnp.int32, (c, c), 1)
    causal = ii >= jj
    strict = ii > jj

    beta_all = _sigmoid(small)
    gc_all = _chunk_cumsum(-jnp.exp(alog) * _softplus(small + dt), 0)
    gc_t = _chunk_cumsum(-jnp.exp(alog_t) * _softplus(small_t + dt_t), 1)

    qs, ks, kbs, decays, rhss, qds, kds, carries = [], [], [], [], [], [], [], []
    for ci in range(tc // c):
        r0 = ci * c
        prev = halo_ref[...] if ci == 0 else p_ref[r0 - HALO:r0, :3 * width]
        ext = jnp.concatenate([prev, p_ref[r0:r0 + c, :3 * width]], axis=0)
        qkv = _silu(_causal_conv(ext, cw, GDN_CONV))
        for h in range(N_HEADS):
            q = qkv[:, h * hd:(h + 1) * hd]
            k = qkv[:, width + h * hd:width + (h + 1) * hd]
            v = qkv[:, 2 * width + h * hd:2 * width + (h + 1) * hd]
            q = q * lax.rsqrt(jnp.sum(q * q, axis=-1, keepdims=True) + EPS) * (hd ** -0.5)
            k = k * lax.rsqrt(jnp.sum(k * k, axis=-1, keepdims=True) + EPS)
            beta = beta_all[r0:r0 + c, h:h + 1]
            gc_col = gc_all[r0:r0 + c, N_HEADS + h:N_HEADS + h + 1]
            gc_row = gc_t[N_HEADS + h:N_HEADS + h + 1, r0:r0 + c]
            exp_gc = jnp.exp(gc_col)
            gc_last = gc_col[c - 1:c, :]
            k_beta = k * beta
            qs.append(q)
            ks.append(k)
            kbs.append(k_beta)
            decays.append(jnp.where(causal, jnp.exp(jnp.where(causal, gc_col - gc_row, 0.0)), 0.0))
            rhss.append(jnp.concatenate([v * beta, k_beta * exp_gc], axis=1))
            qds.append(q * exp_gc)
            kds.append(k * jnp.exp(gc_last - gc_col))
            carries.append(jnp.exp(gc_last))
    q_all, k_all, decay = jnp.stack(qs), jnp.stack(ks), jnp.stack(decays)
    low = jnp.where(strict, _bmm_nt(jnp.stack(kbs), k_all) * decay, 0.0)
    uw = _bmm(_inv_unit_lower(low, ii, jj), jnp.stack(rhss))
    attn = _bmm_nt(q_all, k_all) * decay

    state = state_ref[...]
    for ci in range(tc // c):
        r0 = ci * c
        sl = slice(ci * N_HEADS, (ci + 1) * N_HEADS)
        wq = _bmm(jnp.concatenate([uw[sl, :, hd:], jnp.stack(qds[sl])], axis=1), state)
        v_new = uw[sl, :, :hd] - wq[:, :c]
        o = wq[:, c:] + _bmm(attn[sl], v_new)
        state = jnp.stack([state[h] * carries[ci * N_HEADS + h]
                           + _mm_tn(kds[ci * N_HEADS + h], v_new[h]) for h in range(N_HEADS)])
        for h in range(N_HEADS):
            zg = _silu(p_ref[r0:r0 + c, 3 * width + h * hd:3 * width + (h + 1) * hd])
            o_ref[r0:r0 + c, h * hd:(h + 1) * hd] = (_rms_rows(o[h], gain) * zg).astype(BF16)
    state_ref[...] = state
    halo_ref[...] = p_ref[tc - HALO:tc, :3 * width]


def _hgrn_block(p_ref, lbp, gain, o_ref, state_ref, tc, layer, out_col):
    c = HGRN_CHUNK
    hd = HEAD_DIM
    sup = HGRN_SUPER
    width = N_HEADS * hd
    n_sub = sup // c

    lbe = jnp.exp(lbp - jnp.max(lbp, axis=0, keepdims=True))
    lbs = lbe / jnp.sum(lbe, axis=0, keepdims=True)
    lb_all = jnp.sum(lbs[0:layer + 1], axis=0, keepdims=True) - lbs[0:1]

    half = c // 2
    pos = lax.broadcasted_iota(jnp.int32, (sup, hd), 0) % c
    tile_row = lax.broadcasted_iota(jnp.int32, (half, hd), 0)
    n_pair = half * half + half * c
    gi = lax.broadcasted_iota(jnp.int32, (c, n_pair), 0)
    gj = lax.broadcasted_iota(jnp.int32, (c, n_pair), 1)
    owner = jnp.where(gj < half * half, gj // half, half + (gj - half * half) // c)
    group = jnp.broadcast_to((owner == gi).astype(BF16), (n_sub, c, n_pair))
    ones = jnp.ones((hd, hd), BF16)

    for h in range(N_HEADS):
        lb = lb_all[:, h * hd:(h + 1) * hd]
        st = state_ref[h]
        for r0 in range(0, tc, sup):
            col = lambda part: slice(part * width + h * hd, part * width + (h + 1) * hd)
            f = lb + (1.0 - lb) * _sigmoid(p_ref[r0:r0 + sup, col(1)])
            log_f = jnp.log(jnp.maximum(f, F_FLOOR))
            kk = 1.0 - f
            q = _silu(p_ref[r0:r0 + sup, col(0)])
            v = p_ref[r0:r0 + sup, col(2)]
            b = log_f
            tail = log_f
            s = 1
            while s < c:
                b = b + jnp.where(pos >= s, pltpu.roll(b, s, 0), 0.0)
                tail = tail + jnp.where(pos < c - s, pltpu.roll(tail, sup - s, 0), 0.0)
                s *= 2
            e_b = jnp.exp(b)
            q_dec = q * e_b
            k_dec = kk * jnp.exp(tail - log_f)
            pairs, v_reps = [], []
            for sc in range(n_sub):
                lo = sc * c
                b_a, b_b = b[lo:lo + half], b[lo + half:lo + c]
                k_a, k_b = kk[lo:lo + half], kk[lo + half:lo + c]
                v_a = v[lo:lo + half]
                v_c = v[lo:lo + c]
                for i in range(c):
                    b_i = b[lo + i:lo + i + 1]
                    q_i = q[lo + i:lo + i + 1]
                    if i < half:
                        pairs.append(jnp.exp(jnp.where(tile_row <= i, b_i - b_a, -1e30)) * k_a * q_i)
                        v_reps.append(v_a)
                    else:
                        pairs.append(jnp.exp(b_i - b_a) * k_a * q_i)
                        pairs.append(jnp.exp(jnp.where(tile_row <= i - half, b_i - b_b, -1e30))
                                     * k_b * q_i)
                        v_reps.append(v_c)
            scores = _mm(jnp.concatenate(pairs, axis=0), ones)
            weighted = (scores * jnp.concatenate(v_reps, axis=0)).reshape(n_sub, n_pair, hd)
            o_intra = _bmm(group, weighted)
            updates = [_mm_tn(v[sc * c:(sc + 1) * c], k_dec[sc * c:(sc + 1) * c])
                       for sc in range(n_sub)]
            states = [st]
            for sc in range(n_sub):
                states.append(states[sc] * e_b[sc * c + c - 1:sc * c + c] + updates[sc])
            o = jnp.concatenate([o_intra[sc] + _mm_nt(q_dec[sc * c:(sc + 1) * c], states[sc])
                                 for sc in range(n_sub)], axis=0)
            st = states[n_sub]
            gate = _silu(p_ref[r0:r0 + sup, col(3)])
            o_ref[r0:r0 + sup, out_col + h * hd:out_col + (h + 1) * hd] = (
                _rms_rows(o, gain) * gate).astype(BF16)
        state_ref[h] = st


def _even_kernel(x_ref, g_ref, wa_ref, wb_ref, ws_ref, wst_ref, cw_ref, alog_ref, dt_ref,
                 alog_t_ref, dt_t_ref, ggain_ref, lbp_ref, hgain_ref, o_ref,
                 pa_ref, pb_ref, gstate_ref, ghalo_ref, hstate_ref, *, tc, layer):
    @pl.when(pl.program_id(1) == 0)
    def _():
        gstate_ref[...] = jnp.zeros_like(gstate_ref)
        ghalo_ref[...] = jnp.zeros_like(ghalo_ref)
        hstate_ref[...] = jnp.zeros_like(hstate_ref)

    hn = _rms_rows(x_ref[...], g_ref[...]).astype(BF16)
    pa_ref[...] = jnp.dot(hn, wa_ref[...], preferred_element_type=F32)
    pb_ref[...] = jnp.dot(hn, wb_ref[...], preferred_element_type=F32)
    small = jnp.dot(hn, ws_ref[...], preferred_element_type=F32)
    small_t = _mm_nt(wst_ref[...], hn)
    _gdn_block(pa_ref, small, small_t, cw_ref[...], alog_ref[...], dt_ref[...], alog_t_ref[...],
               dt_t_ref[...], ggain_ref[...], o_ref, gstate_ref, ghalo_ref, tc)
    _hgrn_block(pb_ref, lbp_ref[...], hgain_ref[...], o_ref, hstate_ref, tc, layer,
                N_HEADS * HEAD_DIM)


def _split_even_in_proj(w_in):
    split = 4 * N_HEADS * HEAD_DIM
    return (w_in[:, :, :split].astype(BF16), w_in[:, :, split + 2 * N_HEADS:].astype(BF16),
            w_in[:, :, split:split + 2 * N_HEADS])


def _even_mixer(x, gain, w_a, w_b, w_gates, conv_w, a_log, dt_bias, gdn_gain, lower_bounds_raw,
                hgrn_gain, *, layer, batch, seq, tc=512):
    m, d = x.shape
    nt = seq // tc
    width = N_HEADS * HEAD_DIM
    split = 4 * width
    n_t = 2 * HALO
    w_small = jnp.pad(w_gates, ((0, 0), (0, HEAD_DIM - 2 * N_HEADS))).astype(BF16)
    w_small_t = jnp.pad(w_gates.T, ((0, n_t - 2 * N_HEADS), (0, 0))).astype(BF16)
    lane_vec = lambda v: jnp.pad(v.astype(F32), (N_HEADS, HEAD_DIM - 2 * N_HEADS)).reshape(1, HEAD_DIM)
    row_vec = lambda v: jnp.pad(v.astype(F32), (N_HEADS, n_t - 2 * N_HEADS)).reshape(n_t, 1)
    row = lambda b, t: (b * nt + t, 0)
    const = lambda b, t: (0, 0)
    return pl.pallas_call(
        functools.partial(_even_kernel, tc=tc, layer=layer),
        grid=(batch, nt),
        in_specs=[pl.BlockSpec((tc, d), row),
                  pl.BlockSpec((1, d), const),
                  pl.BlockSpec((None, d, split), lambda b, t: (layer, 0, 0), **RESIDENT),
                  pl.BlockSpec((None, d, split), lambda b, t: (layer, 0, 0), **RESIDENT),
                  pl.BlockSpec((d, HEAD_DIM), const),
                  pl.BlockSpec((n_t, d), const),
                  pl.BlockSpec((GDN_CONV, 3 * width), const),
                  pl.BlockSpec((1, HEAD_DIM), const),
                  pl.BlockSpec((1, HEAD_DIM), const),
                  pl.BlockSpec((n_t, 1), const),
                  pl.BlockSpec((n_t, 1), const),
                  pl.BlockSpec((1, HEAD_DIM), const),
                  pl.BlockSpec(lower_bounds_raw.shape, const),
                  pl.BlockSpec((1, HEAD_DIM), const)],
        out_specs=pl.BlockSpec((tc, 2 * width), row),
        out_shape=jax.ShapeDtypeStruct((m, 2 * width), BF16),
        scratch_shapes=[pltpu.VMEM((tc, split), F32),
                        pltpu.VMEM((tc, split), F32),
                        pltpu.VMEM((N_HEADS, HEAD_DIM, HEAD_DIM), F32),
                        pltpu.VMEM((HALO, 3 * width), F32),
                        pltpu.VMEM((N_HEADS, HEAD_DIM, HEAD_DIM), F32)],
        compiler_params=_params("parallel", "arbitrary"),
        name="even_mixer",
    )(x, gain.reshape(1, d), w_a, w_b, w_small, w_small_t, conv_w, lane_vec(a_log),
      lane_vec(dt_bias), row_vec(a_log), row_vec(dt_bias), gdn_gain.reshape(1, HEAD_DIM),
      lower_bounds_raw, hgrn_gain.reshape(1, HEAD_DIM))


def _rglru_kernel(res_ref, g_ref, win_ref, cw_ref, cb_ref, wa_ref, ba_ref, wx_ref, bx_ref,
                  lam_ref, o_ref, h_ref, halo_ref, *, tc):
    n_heads = wa_ref.shape[0]
    blk = LRU_BLOCK
    width = n_heads * blk

    @pl.when(pl.program_id(1) == 0)
    def _():
        h_ref[...] = jnp.zeros_like(h_ref)
        halo_ref[...] = jnp.zeros_like(halo_ref)

    hn = _rms_rows(res_ref[...], g_ref[...]).astype(BF16)
    y = jnp.dot(hn, win_ref[:, :width], preferred_element_type=F32)
    x = jnp.dot(hn, win_ref[:, width:], preferred_element_type=F32)
    ext = jnp.concatenate([halo_ref[...], x], axis=0)
    halo_ref[...] = x[tc - HALO:tc]
    xc = _causal_conv(ext, cw_ref[...], LRU_CONV) + cb_ref[...]
    ra = jnp.concatenate([_mm(xc[:, h * blk:(h + 1) * blk], wa_ref[h]) for h in range(n_heads)],
                         axis=1)
    rx = jnp.concatenate([_mm(xc[:, h * blk:(h + 1) * blk], wx_ref[h]) for h in range(n_heads)],
                         axis=1)
    r = _sigmoid(ra + ba_ref[...])
    gi = _sigmoid(rx + bx_ref[...])
    log_a = r * (-RG_C * _softplus(-lam_ref[...]))
    a = jnp.exp(log_a)
    th = jnp.tanh(log_a)
    one_minus_a2 = -2.0 * th / (1.0 - th)
    root = jnp.where(one_minus_a2 > 0.0, one_minus_a2 * lax.rsqrt(one_minus_a2), 0.0)
    u = root * (gi * xc)

    first = lax.broadcasted_iota(jnp.int32, (HALO, width), 0) == 0
    carry = h_ref[0:1, :]
    tiles = []
    for t in range(tc // HALO):
        a_t = a[t * HALO:(t + 1) * HALO]
        ht = u[t * HALO:(t + 1) * HALO] + jnp.where(first, a_t * carry, 0.0)
        mult = jnp.where(first, 0.0, a_t)
        for s in (1, 2, 4):
            ht = ht + mult * pltpu.roll(ht, s, 0)
            if s < HALO // 2:
                mult = mult * pltpu.roll(mult, s, 0)
        tiles.append(ht)
        carry = ht[HALO - 1:HALO]
    h_ref[...] = jnp.broadcast_to(carry, h_ref.shape)
    hs = jnp.concatenate(tiles, axis=0)
    half_y = 0.5 * y
    gate = half_y + half_y * jnp.tanh(y * (GELU_C + (GELU_C * 0.044715) * (y * y)))
    o_ref[...] = (hs * gate).astype(BF16)


def _rglru(x, gain, w_in, conv_w, conv_b, wa, ba, wx, bx, lam, *, layer, batch, seq, tc=1024):
    m, d = x.shape
    nt = seq // tc
    width = conv_w.shape[1]
    const = lambda b, t: (0, 0)
    row = lambda b, t: (b * nt + t, 0)
    vec = pl.BlockSpec((1, width), const)
    mat = pl.BlockSpec((None,) + wa.shape[1:], lambda b, t: (layer, 0, 0, 0))
    return pl.pallas_call(
        functools.partial(_rglru_kernel, tc=tc),
        grid=(batch, nt),
        in_specs=[pl.BlockSpec((tc, d), row),
                  pl.BlockSpec((1, d), const),
                  pl.BlockSpec((None,) + w_in.shape[1:], lambda b, t: (layer, 0, 0), **RESIDENT),
                  pl.BlockSpec((LRU_CONV, width), const),
                  vec, mat, vec, mat, vec, vec],
        out_specs=pl.BlockSpec((tc, width), row),
        out_shape=jax.ShapeDtypeStruct((m, width), BF16),
        scratch_shapes=[pltpu.VMEM((HALO, width), F32), pltpu.VMEM((HALO, width), F32)],
        compiler_params=_params("parallel", "arbitrary"),
        name="rglru",
    )(x, gain.reshape(1, d), w_in, conv_w, conv_b.reshape(1, width), wa, ba.reshape(1, width),
      wx, bx.reshape(1, width), lam.reshape(1, width))


def _ffn_kernel(x_ref, mix_ref, wmix_ref, g_ref, wu_ref, cw_ref, cb_ref, wd_ref, gf_ref, o_ref,
                xm_ref, h_ref, act_ref, halo_ref, *, tiles_per_seq, tf, final_norm):
    d_ff = wd_ref.shape[0]

    @pl.when(pl.program_id(0) % tiles_per_seq == 0)
    def _():
        halo_ref[...] = jnp.zeros_like(halo_ref)

    xm = x_ref[...] + jnp.dot(mix_ref[...], wmix_ref[...], preferred_element_type=F32)
    xm_ref[...] = xm
    h_ref[...] = _rms_rows(xm, g_ref[...]).astype(BF16)
    for c0 in range(0, d_ff, tf):
        c1 = min(c0 + tf, d_ff)
        h = h_ref[...]
        gate = jnp.dot(h, wu_ref[:, c0:c1], preferred_element_type=F32)
        val = jnp.dot(h, wu_ref[:, d_ff + c0:d_ff + c1], preferred_element_type=F32)
        ext = jnp.concatenate([halo_ref[:, c0:c1], gate], axis=0)
        halo_ref[:, c0:c1] = gate[gate.shape[0] - HALO:]
        gc = _causal_conv(ext, cw_ref[:, c0:c1], FFN_CONV) + cb_ref[:, c0:c1]
        act_ref[:, c0:c1] = (_silu(gc) * val).astype(BF16)
    acc = xm_ref[...] + jnp.dot(act_ref[...], wd_ref[...], preferred_element_type=F32)
    if final_norm:
        acc = _rms_rows(acc, gf_ref[...])
    o_ref[...] = acc


def _mix_ffn(x, mix, w_mix, mix_layer, gain, w_up, conv_w, conv_b, w_down, final_gain, *, layer,
             seq, final_norm, tm=1024, tf=1024):
    m, d = x.shape
    d_ff = w_down.shape[1]
    row = lambda i: (i, 0)
    const = lambda i: (0, 0)
    return pl.pallas_call(
        functools.partial(_ffn_kernel, tiles_per_seq=seq // tm, tf=tf, final_norm=final_norm),
        grid=(m // tm,),
        in_specs=[pl.BlockSpec((tm, d), row),
                  pl.BlockSpec((tm, mix.shape[1]), row),
                  pl.BlockSpec((None,) + w_mix.shape[1:], lambda i: (mix_layer, 0, 0), **RESIDENT),
                  pl.BlockSpec((1, d), const),
                  pl.BlockSpec((None, d, 2 * d_ff), lambda i: (layer, 0, 0), **RESIDENT),
                  pl.BlockSpec((FFN_CONV, d_ff), const),
                  pl.BlockSpec((1, d_ff), const),
                  pl.BlockSpec((None, d_ff, d), lambda i: (layer, 0, 0), **RESIDENT),
                  pl.BlockSpec((1, d), const)],
        out_specs=pl.BlockSpec((tm, d), row),
        out_shape=jax.ShapeDtypeStruct((m, d), F32),
        scratch_shapes=[pltpu.VMEM((tm, d), F32),
                        pltpu.VMEM((tm, d), BF16),
                        pltpu.VMEM((tm, d_ff), BF16),
                        pltpu.VMEM((HALO, d_ff), F32)],
        compiler_params=_params("arbitrary"),
        name="mix_ffn",
    )(x, mix, w_mix, gain.reshape(1, d), w_up, conv_w, conv_b.reshape(1, d_ff), w_down,
      final_gain.reshape(1, d))


def kernel(x, norm_mix, norm_ffn, norm_final, ab_w_in, gdn_conv_w, gdn_a_log, gdn_dt_bias,
           gdn_norm, hgrn_lower_bounds, hgrn_norm, ab_w_out, c_w_in, c_conv_w, c_conv_b,
           c_gate_a_w, c_gate_a_b, c_gate_x_w, c_gate_x_b, c_lambda, c_w_out,
           ffn_w_up, ffn_conv_w, ffn_conv_b, ffn_w_down):
    batch, seq, d = x.shape
    depth = norm_mix.shape[0]
    x = x.reshape(batch * seq, d)
    ab_w_a, ab_w_b, ab_w_gates = _split_even_in_proj(ab_w_in)
    ab_w_out, c_w_in, c_w_out = (w.astype(BF16) for w in (ab_w_out, c_w_in, c_w_out))
    c_gate_a_w, c_gate_x_w = c_gate_a_w.astype(BF16), c_gate_x_w.astype(BF16)
    ffn_w_up, ffn_w_down = ffn_w_up.astype(BF16), ffn_w_down.astype(BF16)
    for layer in range(depth):
        j = layer // 2
        if layer % 2 == 0:
            mix = _even_mixer(x, norm_mix[layer], ab_w_a, ab_w_b, ab_w_gates[j], gdn_conv_w[j],
                              gdn_a_log[j], gdn_dt_bias[j], gdn_norm[j], hgrn_lower_bounds,
                              hgrn_norm[j], layer=j, batch=batch, seq=seq)
            w_mix = ab_w_out
        else:
            mix = _rglru(x, norm_mix[layer], c_w_in, c_conv_w[j], c_conv_b[j], c_gate_a_w,
                         c_gate_a_b[j], c_gate_x_w, c_gate_x_b[j], c_lambda[j], layer=j,
                         batch=batch, seq=seq)
            w_mix = c_w_out
        x = _mix_ffn(x, mix, w_mix, j, norm_ffn[layer], ffn_w_up, ffn_conv_w[layer],
                     ffn_conv_b[layer], ffn_w_down, norm_final, layer=layer, seq=seq,
                     final_norm=layer == depth - 1)
    return x.reshape(batch, seq, d)
```
